```python
import jax, jax.numpy as jnp
from jax import lax
import numpy as np

D_MODEL = 2048
BATCH = 1
SEQ = 8192
DEPTH = 4

HEAD_DIM = 128
N_RET_HEADS = 6
N_ATT_HEADS = 6
N_GMLP_GROUPS = 4
RET_WIDTH = N_RET_HEADS * HEAD_DIM
ATT_WIDTH = N_ATT_HEADS * HEAD_DIM
GMLP_WIDTH = N_GMLP_GROUPS * HEAD_DIM
MIX_WIDTH = RET_WIDTH + ATT_WIDTH + GMLP_WIDTH
IN_SPLITS = (RET_WIDTH, RET_WIDTH, RET_WIDTH, RET_WIDTH,
             ATT_WIDTH, ATT_WIDTH, ATT_WIDTH,
             GMLP_WIDTH, GMLP_WIDTH)
IN_WIDTH = sum(IN_SPLITS)
RET_CHUNK = 128
ATT_BLOCK = 128
GMLP_CHUNK = 128
DILATED_PATTERNS = ((128, 1), (512, 4), (2048, 16))
D_FF = 5632
CONV_WIDTH = 3
ROPE_BASE = 10000.0
EPS = 1e-6
NEG_INF = -1e30

kernel_name = "hybrid_retention_dilated_gmlp_trunk"


def rms_norm(x, w):
    xf = x.astype(jnp.float32)
    y = xf * lax.rsqrt(jnp.mean(xf * xf, axis=-1, keepdims=True) + EPS)
    return (y * w.astype(jnp.float32)).astype(x.dtype)


def layer_norm(x, w):
    xf = x.astype(jnp.float32)
    mu = jnp.mean(xf, axis=-1, keepdims=True)
    xc = xf - mu
    y = xc * lax.rsqrt(jnp.mean(xc * xc, axis=-1, keepdims=True) + EPS)
    return (y * w.astype(jnp.float32)).astype(x.dtype)


def rotary(x, pos):
    half = x.shape[-1] // 2
    inv_freq = ROPE_BASE ** (-jnp.arange(half, dtype=jnp.float32) / half)
    ang = pos.astype(jnp.float32)[:, None] * inv_freq[None, :]
    cos = jnp.cos(ang)[None, :, None, :]
    sin = jnp.sin(ang)[None, :, None, :]
    x1 = x[..., :half].astype(jnp.float32)
    x2 = x[..., half:].astype(jnp.float32)
    return jnp.concatenate([x1 * cos - x2 * sin, x1 * sin + x2 * cos], axis=-1)


def retention(q, k, v, pos):
    B, S, H, Dh = q.shape
    C = RET_CHUNK
    N = S // C
    q = rotary(q, pos)
    k = rotary(k, pos) * (Dh ** -0.5)
    v = v.astype(jnp.float32)
    log_g = jnp.log1p(-jnp.exp2(-5.0 - jnp.arange(H, dtype=jnp.float32)))
    idx = jnp.arange(C, dtype=jnp.float32)
    diff = idx[:, None] - idx[None, :]
    decay_in = jnp.where(diff[None] >= 0,
                         jnp.exp(jnp.maximum(diff, 0.0)[None] * log_g[:, None, None]), 0.0)
    xi = jnp.exp((idx[:, None] + 1.0) * log_g[None, :])
    zeta = jnp.exp((C - 1.0 - idx)[:, None] * log_g[None, :])
    chunk_decay = jnp.exp(C * log_g)
    qc = q.reshape(B, N, C, H, Dh)
    kc = k.reshape(B, N, C, H, Dh)
    vc = v.reshape(B, N, C, H, Dh)
    scores = jnp.einsum('bnqhd,bnkhd->bnhqk', qc, kc) * decay_in
    inner = jnp.einsum('bnhqk,bnkhe->bnqhe', scores, vc)
    upd = jnp.einsum('bnkhd,bnkhe->nbhde', kc * zeta[:, :, None], vc)

    def step(state, u):
        return chunk_decay[None, :, None, None] * state + u, state

    _, state_prev = lax.scan(step, jnp.zeros((B, H, Dh, Dh), jnp.float32), upd)
    cross = jnp.einsum('bnqhd,nbhde->bnqhe', qc * xi[:, :, None], state_prev)
    return (inner + cross).reshape(B, S, H, Dh)


def dilated_branch(q, k, v, window, dilation):
    B, S, H, Dh = q.shape
    blk = ATT_BLOCK
    steps = window // dilation
    span = dilation * blk
    s_pad = -(-S // span) * span
    L = s_pad // dilation
    nb = L // blk

    def to_blocks(t):
        t = jnp.pad(t.astype(jnp.float32), ((0, 0), (0, s_pad - S), (0, 0), (0, 0)))
        t = t.reshape(B, L, dilation, H, Dh).transpose(0, 3, 2, 1, 4)
        return t.reshape(B, H, dilation, nb, blk, Dh)

    def with_prev(t):
        prev = jnp.pad(t, ((0, 0), (0, 0), (0, 0), (1, 0), (0, 0), (0, 0)))[:, :, :, :-1]
        return jnp.concatenate([prev, t], axis=4)

    qb = to_blocks(q)
    kk = with_prev(to_blocks(k))
    vv = with_prev(to_blocks(v))
    scores = jnp.einsum('bhrnqd,bhrnkd->bhrnqk', qb, kk) * (Dh ** -0.5)
    a = jnp.arange(blk)[:, None]
    c = jnp.arange(2 * blk)[None, :]
    dist = a + blk - c
    band = (dist >= 0) & (dist <= steps)
    mask = band[None] & ((jnp.arange(nb)[:, None, None] > 0) | (c >= blk)[None])
    scores = jnp.where(mask, scores, NEG_INF)
    m = jnp.max(scores, axis=-1, keepdims=True)
    p = jnp.exp(scores - m)
    den = jnp.sum(p, axis=-1, keepdims=True)
    out = jnp.einsum('bhrnqk,bhrnkd->bhrnqd', p, vv) / den
    lse = m[..., 0] + jnp.log(den[..., 0])
    out = out.reshape(B, H, dilation, L, Dh).transpose(0, 3, 2, 1, 4).reshape(B, s_pad, H, Dh)[:, :S]
    lse = lse.reshape(B, H, dilation, L).transpose(0, 3, 2, 1).reshape(B, s_pad, H)[:, :S]
    return out, lse


def dilated_attention(q, k, v):
    outs, lses = [], []
    for window, dilation in DILATED_PATTERNS:
        o, l = dilated_branch(q, k, v, window, dilation)
        outs.append(o)
        lses.append(l)
    wts = jax.nn.softmax(jnp.stack(lses, axis=0), axis=0)
    return jnp.einsum('pbsh,pbshd->bshd', wts, jnp.stack(outs, axis=0))


def chunk_gmlp(u, v, ln_w, ws, bs):
    B, S, _ = u.shape
    C = GMLP_CHUNK
    G = N_GMLP_GROUPS
    N = S // C
    u = jax.nn.gelu(u)
    v = layer_norm(jax.nn.gelu(v), ln_w)
    tri = jnp.tril(jnp.ones((C, C), dtype=bool))
    wm = jnp.where(tri[None], ws, jnp.zeros_like(ws))
    vc = v.reshape(B, N, C, G, HEAD_DIM)
    sp = jnp.einsum('gqk,bnkgc->bnqgc', wm, vc) + bs.T[None, None, :, :, None]
    return u * sp.reshape(B, S, GMLP_WIDTH)


def conv_ffn(x, w_up, conv_w, conv_b, w_down):
    S = x.shape[1]
    h = x @ w_up
    hp = jnp.pad(h, ((0, 0), (CONV_WIDTH - 1, 0), (0, 0)))
    acc = conv_b + conv_w[0] * hp[:, 0:S]
    for j in range(1, CONV_WIDTH):
        acc = acc + conv_w[j] * hp[:, j:j + S]
    gate, val = jnp.split(acc, 2, axis=-1)
    return (jax.nn.silu(gate) * val) @ w_down


def hybrid_layer(x, pos, norm1_w, w_in, ret_norm_w, att_norm_w, gmlp_ln_w, gmlp_ws, gmlp_bs,
                 gmlp_out_w, w_out, norm2_w, w_up, conv_w, conv_b, w_down):
    B, S, _ = x.shape
    h = rms_norm(x, norm1_w)
    proj = h @ w_in
    offsets = [int(o) for o in np.cumsum(IN_SPLITS)[:-1]]
    rq, rk, rv, rg, aq, ak, av, gu, gv = jnp.split(proj, offsets, axis=-1)

    def heads(t):
        return t.reshape(B, S, -1, HEAD_DIM)

    ret = retention(heads(rq), heads(rk), heads(rv), pos)
    ret = rms_norm(ret, ret_norm_w.reshape(N_RET_HEADS, HEAD_DIM)).reshape(B, S, RET_WIDTH)
    ret = jax.nn.silu(rg.astype(jnp.float32)) * ret
    att = dilated_attention(heads(aq), heads(ak), heads(av)).reshape(B, S, ATT_WIDTH)
    att = rms_norm(att, att_norm_w)
    gm = rms_norm(chunk_gmlp(gu, gv, gmlp_ln_w, gmlp_ws, gmlp_bs), gmlp_out_w)

    mixed = jnp.concatenate([ret.astype(x.dtype), att.astype(x.dtype), gm.astype(x.dtype)], axis=-1)
    x = x + mixed @ w_out
    x = x + conv_ffn(rms_norm(x, norm2_w), w_up, conv_w, conv_b, w_down)
    return x


def setup_inputs(seed: int = 0) -> dict:
    key = jax.random.key(seed)
    ks = jax.random.split(key, 16)
    f32 = jnp.float32

    def gain(k, shape):
        return 1.0 + 0.02 * jax.random.normal(k, shape, f32)

    return {
        "x": jax.random.normal(ks[0], (BATCH, SEQ, D_MODEL), f32),
        "norm1_w": gain(ks[1], (DEPTH, D_MODEL)),
        "w_in": jax.random.normal(ks[2], (DEPTH, D_MODEL, IN_WIDTH), f32) * D_MODEL ** -0.5,
        "ret_norm_w": gain(ks[3], (DEPTH, RET_WIDTH)),
        "att_norm_w": gain(ks[4], (DEPTH, ATT_WIDTH)),
        "gmlp_ln_w": gain(ks[5], (DEPTH, GMLP_WIDTH)),
        "gmlp_ws": jax.random.normal(ks[6], (DEPTH, N_GMLP_GROUPS, GMLP_CHUNK, GMLP_CHUNK), f32) * GMLP_CHUNK ** -0.5,
        "gmlp_bs": 1.0 + 0.01 * jax.random.normal(ks[7], (DEPTH, N_GMLP_GROUPS, GMLP_CHUNK), f32),
        "gmlp_out_w": gain(ks[8], (DEPTH, GMLP_WIDTH)),
        "w_out": jax.random.normal(ks[9], (DEPTH, MIX_WIDTH, D_MODEL), f32) * MIX_WIDTH ** -0.5,
        "norm2_w": gain(ks[10], (DEPTH, D_MODEL)),
        "w_up": jax.random.normal(ks[11], (DEPTH, D_MODEL, 2 * D_FF), f32) * D_MODEL ** -0.5,
        "conv_w": jax.random.normal(ks[12], (DEPTH, CONV_WIDTH, 2 * D_FF), f32) * CONV_WIDTH ** -0.5,
        "conv_b": 0.01 * jax.random.normal(ks[13], (DEPTH, 2 * D_FF), f32),
        "w_down": jax.random.normal(ks[14], (DEPTH, D_FF, D_MODEL), f32) * D_FF ** -0.5,
        "final_norm_w": gain(ks[15], (D_MODEL,)),
    }


def reference(x, norm1_w, w_in, ret_norm_w, att_norm_w, gmlp_ln_w, gmlp_ws, gmlp_bs, gmlp_out_w,
              w_out, norm2_w, w_up, conv_w, conv_b, w_down, final_norm_w):
    pos = jnp.arange(x.shape[1], dtype=jnp.int32)
    for l in range(DEPTH):
        x = hybrid_layer(x, pos, norm1_w[l], w_in[l], ret_norm_w[l], att_norm_w[l], gmlp_ln_w[l],
                         gmlp_ws[l], gmlp_bs[l], gmlp_out_w[l], w_out[l], norm2_w[l], w_up[l],
                         conv_w[l], conv_b[l], w_down[l])
    return rms_norm(x, final_norm_w)
```

```python
import functools

import jax
import jax.numpy as jnp
from jax import lax
from jax.experimental import pallas as pl
from jax.experimental.pallas import tpu as pltpu

F32 = jnp.float32
BF16 = jnp.bfloat16

LANES = 128
SUBLANES = 8
CHUNK = 128
N_RET = 6
N_ATT = 6
N_GMLP = 4
DILATIONS = (1, 4, 16)
SPAN = CHUNK * DILATIONS[-1]
ROPE_BASE = 10000.0
EPS = 1e-6
NEG_INF = -1e30
VMEM_LIMIT = 56 * 1024 * 1024

RQ, RK, RV, RG = 0, N_RET, 2 * N_RET, 3 * N_RET
AQ = 4 * N_RET
AK = AQ + N_ATT
AV = AK + N_ATT
GU = AV + N_ATT
GV = GU + N_GMLP
N_SLABS = GV + N_GMLP


def _params(*sem):
    return pltpu.CompilerParams(dimension_semantics=sem, vmem_limit_bytes=VMEM_LIMIT)


def _rms(x, w):
    return x * lax.rsqrt(jnp.mean(x * x, axis=-1, keepdims=True) + EPS) * w


def _rmsnorm_kernel(x_ref, w_ref, o_ref):
    o_ref[...] = _rms(x_ref[...], w_ref[...]).astype(o_ref.dtype)


def _rmsnorm(x, w, tm=512):
    S, D = x.shape
    return pl.pallas_call(
        _rmsnorm_kernel,
        grid=(S // tm,),
        in_specs=[pl.BlockSpec((tm, D), lambda i: (i, 0)), pl.BlockSpec((1, D), lambda i: (0, 0))],
        out_specs=pl.BlockSpec((tm, D), lambda i: (i, 0)),
        out_shape=jax.ShapeDtypeStruct((S, D), BF16),
        compiler_params=_params("parallel"),
        name="rmsnorm",
    )(x, w.reshape(1, D))


def _inproj_kernel(a_ref, w_ref, o_ref):
    acc = jnp.dot(a_ref[...], w_ref[...], preferred_element_type=F32)
    for j in range(o_ref.shape[0]):
        o_ref[j] = acc[:, j * LANES:(j + 1) * LANES]


def _inproj(h, w, tm=1024, tn=1280):
    S, D = h.shape
    N = w.shape[1]
    return pl.pallas_call(
        _inproj_kernel,
        grid=(N // tn, S // tm),
        in_specs=[pl.BlockSpec((tm, D), lambda j, i: (i, 0)), pl.BlockSpec((D, tn), lambda j, i: (0, j))],
        out_specs=pl.BlockSpec((tn // LANES, tm, LANES), lambda j, i: (j, i, 0)),
        out_shape=jax.ShapeDtypeStruct((N // LANES, S, LANES), F32),
        compiler_params=_params("parallel", "parallel"),
        name="inproj",
    )(h, w)


def _retention_kernel(q_ref, k_ref, v_ref, g_ref, cos_ref, sin_ref, dec_ref, xi_ref, zeta_ref, cd_ref,
                      nw_ref, o_ref, state_ref):
    @pl.when(pl.program_id(1) == 0)
    def _():
        state_ref[...] = jnp.zeros_like(state_ref)

    dec, xi, zeta, cd, nw = dec_ref[...], xi_ref[...], zeta_ref[...], cd_ref[...], nw_ref[...]
    scale = LANES ** -0.5
    for c in range(q_ref.shape[0] // CHUNK):
        rows = pl.ds(c * CHUNK, CHUNK)
        cos, sin = cos_ref[rows, :], sin_ref[rows, :]
        q, k = q_ref[rows, :], k_ref[rows, :]
        qr = q * cos + pltpu.roll(q, LANES // 2, 1) * sin
        kr = (k * cos + pltpu.roll(k, LANES // 2, 1) * sin) * scale
        vb = v_ref[rows, :].astype(BF16)
        s = lax.dot_general(qr.astype(BF16), kr.astype(BF16), (((1,), (1,)), ((), ())),
                            preferred_element_type=F32) * dec
        inner = jnp.dot(s.astype(BF16), vb, preferred_element_type=F32)
        state = state_ref[...]
        cross = jnp.dot((qr * xi).astype(BF16), state.astype(BF16), preferred_element_type=F32)
        upd = lax.dot_general((kr * zeta).astype(BF16), vb, (((0,), (0,)), ((), ())),
                              preferred_element_type=F32)
        state_ref[...] = cd * state + upd
        g = g_ref[rows, :]
        o_ref[rows, :] = (g * jax.nn.sigmoid(g)) * _rms(inner + cross, nw)


def _retention_tables(S):
    half = LANES // 2
    inv_freq = ROPE_BASE ** (-jnp.arange(half, dtype=F32) / half)
    ang = jnp.arange(S, dtype=jnp.int32).astype(F32)[:, None] * inv_freq[None, :]
    cos, sin = jnp.cos(ang), jnp.sin(ang)
    cosf = jnp.concatenate([cos, cos], axis=-1)
    sinf = jnp.concatenate([-sin, sin], axis=-1)
    log_g = jnp.log1p(-jnp.exp2(-5.0 - jnp.arange(N_RET, dtype=F32)))
    idx = jnp.arange(CHUNK, dtype=F32)
    diff = idx[:, None] - idx[None, :]
    dec = jnp.where(diff[None] >= 0, jnp.exp(jnp.maximum(diff, 0.0)[None] * log_g[:, None, None]), 0.0)
    xi = jnp.exp((idx[None, :] + 1.0) * log_g[:, None])
    zeta = jnp.exp((CHUNK - 1.0 - idx)[None, :] * log_g[:, None])
    cd = jnp.exp(CHUNK * log_g)
    bcast = lambda t: jnp.broadcast_to(t[:, :, None], (N_RET, CHUNK, LANES))
    return cosf, sinf, dec, bcast(xi), bcast(zeta), jnp.broadcast_to(cd[:, None, None], (N_RET, 1, LANES))


def _retention(proj, tables, norm_w, tr=512):
    S = proj.shape[1]
    cosf, sinf, dec, xi, zeta, cd = tables
    slab = lambda off: pl.BlockSpec((None, tr, LANES), lambda h, i: (off + h, i, 0))
    tab = pl.BlockSpec((tr, LANES), lambda h, i: (i, 0))
    per_head = lambda r: pl.BlockSpec((None, r, LANES), lambda h, i: (h, 0, 0))
    return pl.pallas_call(
        _retention_kernel,
        grid=(N_RET, S // tr),
        in_specs=[slab(RQ), slab(RK), slab(RV), slab(RG), tab, tab,
                  per_head(CHUNK), per_head(CHUNK), per_head(CHUNK), per_head(1), per_head(1)],
        out_specs=pl.BlockSpec((None, tr, LANES), lambda h, i: (h, i, 0)),
        out_shape=jax.ShapeDtypeStruct((N_RET, S, LANES), F32),
        scratch_shapes=[pltpu.VMEM((LANES, LANES), F32)],
        compiler_params=_params("parallel", "arbitrary"),
        name="retention",
    )(proj, proj, proj, proj, cosf, sinf, dec, xi, zeta, cd, norm_w.reshape(N_RET, 1, LANES))


def _rows(start, size, stride):
    return pl.ds(start, size) if stride == 1 else pl.ds(start, size, stride=stride)


def _dilated_kernel(q_ref, k_ref, v_ref, o_ref, kk, vv, o1, o2, o3, l1, l2, l3):
    n = pl.program_id(1)

    @pl.when(n == 0)
    def _():
        kk[0:SPAN, :] = jnp.zeros((SPAN, LANES), F32)
        vv[0:SPAN, :] = jnp.zeros((SPAN, LANES), F32)

    @pl.when(n > 0)
    def _():
        kk[0:SPAN, :] = kk[SPAN:2 * SPAN, :]
        vv[0:SPAN, :] = vv[SPAN:2 * SPAN, :]

    kk[SPAN:2 * SPAN, :] = k_ref[...]
    vv[SPAN:2 * SPAN, :] = v_ref[...]

    a = lax.broadcasted_iota(jnp.int32, (CHUNK, 2 * CHUNK), 0)
    c = lax.broadcasted_iota(jnp.int32, (CHUNK, 2 * CHUNK), 1)
    band = (c >= a) & (c <= a + CHUNK)
    in_block = c >= CHUNK
    scale = LANES ** -0.5

    def unit(qstart, kstart, stride, has_prev, o_s, l_s):
        q = q_ref[_rows(qstart, CHUNK, stride), :]
        k = kk[_rows(kstart, 2 * CHUNK, stride), :]
        v = vv[_rows(kstart, 2 * CHUNK, stride), :]
        s = lax.dot_general(q.astype(BF16), k.astype(BF16), (((1,), (1,)), ((), ())),
                            preferred_element_type=F32) * scale
        s = jnp.where(band & (in_block | has_prev), s, NEG_INF)
        m = jnp.max(s, axis=-1, keepdims=True)
        p = jnp.exp(s - m)
        den = jnp.sum(p, axis=-1, keepdims=True)
        o = jnp.dot(p.astype(BF16), v.astype(BF16), preferred_element_type=F32) / den
        o_s[_rows(qstart, CHUNK, stride), :] = o
        l_s[_rows(qstart, CHUNK, stride), :] = jnp.broadcast_to(m + jnp.log(den), (CHUNK, LANES))

    d1, d2, d3 = DILATIONS
    sub2 = CHUNK * d2
    per2 = SPAN // sub2

    def body(it, carry):
        unit(it * CHUNK, SPAN + (it - 1) * CHUNK, d1, (n > 0) | (it > 0), o1, l1)
        ss, r = it // d2, it % d2
        unit(ss * sub2 + r, SPAN + (ss - 1) * sub2 + r, d2, (n > 0) | (ss > 0), o2, l2)
        unit(it, it, d3, n > 0, o3, l3)
        return carry

    assert per2 * d2 == d3 == SPAN // CHUNK
    lax.fori_loop(0, d3, body, 0)

    def merge(b, carry):
        rows = pl.ds(b * CHUNK, CHUNK)
        a1, a2, a3 = l1[rows, :], l2[rows, :], l3[rows, :]
        mx = jnp.maximum(jnp.maximum(a1, a2), a3)
        w1, w2, w3 = jnp.exp(a1 - mx), jnp.exp(a2 - mx), jnp.exp(a3 - mx)
        o_ref[rows, :] = (w1 * o1[rows, :] + w2 * o2[rows, :] + w3 * o3[rows, :]) / (w1 + w2 + w3)
        return carry

    lax.fori_loop(0, SPAN // CHUNK, merge, 0)


def _dilated_attention(proj):
    S = proj.shape[1]
    slab = lambda off: pl.BlockSpec((None, SPAN, LANES), lambda h, n: (off + h, n, 0))
    big = pltpu.VMEM((2 * SPAN, LANES), F32)
    acc = pltpu.VMEM((SPAN, LANES), F32)
    return pl.pallas_call(
        _dilated_kernel,
        grid=(N_ATT, S // SPAN),
        in_specs=[slab(AQ), slab(AK), slab(AV)],
        out_specs=pl.BlockSpec((None, SPAN, LANES), lambda h, n: (h, n, 0)),
        out_shape=jax.ShapeDtypeStruct((N_ATT, S, LANES), F32),
        scratch_shapes=[big, big, acc, acc, acc, acc, acc, acc],
        compiler_params=_params("parallel", "arbitrary"),
        name="dilated_attention",
    )(proj, proj, proj)


def _gmlp_kernel(*refs):
    G = N_GMLP
    u_refs, v_refs = refs[:G], refs[G:2 * G]
    lnw_ref, ws_ref, bs_ref, ow_ref, o_ref = refs[2 * G:]
    tm = o_ref.shape[1]
    width = G * LANES
    gv = [jax.nn.gelu(r[...]) for r in v_refs]
    mu = sum(jnp.sum(t, axis=-1, keepdims=True) for t in gv) / width
    xc = [t - mu for t in gv]
    inv = lax.rsqrt(sum(jnp.sum(t * t, axis=-1, keepdims=True) for t in xc) / width + EPS)
    qi = lax.broadcasted_iota(jnp.int32, (CHUNK, CHUNK), 0)
    ki = lax.broadcasted_iota(jnp.int32, (CHUNK, CHUNK), 1)
    outs = []
    for g in range(G):
        vn = (xc[g] * inv * lnw_ref[g]).astype(BF16)
        wm = jnp.where(ki <= qi, ws_ref[g], 0.0).astype(BF16)
        sp = [jnp.dot(wm, vn[c * CHUNK:(c + 1) * CHUNK, :], preferred_element_type=F32) + bs_ref[g]
              for c in range(tm // CHUNK)]
        outs.append(jax.nn.gelu(u_refs[g][...]) * jnp.concatenate(sp, axis=0))
    inv_o = lax.rsqrt(sum(jnp.sum(t * t, axis=-1, keepdims=True) for t in outs) / width + EPS)
    for g in range(G):
        o_ref[g] = outs[g] * inv_o * ow_ref[g]


def _gmlp(proj, ln_w, ws, bs, out_w, tm=512):
    S = proj.shape[1]
    G = N_GMLP
    slab = lambda s: pl.BlockSpec((None, tm, LANES), lambda i: (s, i, 0))
    small = lambda shape: pl.BlockSpec(shape, lambda i: (0,) * len(shape))
    return pl.pallas_call(
        _gmlp_kernel,
        grid=(S // tm,),
        in_specs=[slab(GU + g) for g in range(G)] + [slab(GV + g) for g in range(G)]
        + [small((G, 1, LANES)), small((G, CHUNK, CHUNK)), small((G, CHUNK, 1)), small((G, 1, LANES))],
        out_specs=pl.BlockSpec((G, tm, LANES), lambda i: (0, i, 0)),
        out_shape=jax.ShapeDtypeStruct((G, S, LANES), F32),
        compiler_params=_params("parallel"),
        name="gmlp",
    )(*([proj] * (2 * G)), ln_w.reshape(G, 1, LANES), ws, bs.reshape(G, CHUNK, 1), out_w.reshape(G, 1, LANES))


def _outproj_kernel(ret_ref, att_ref, gm_ref, x_ref, attw_ref, w_ref, n2w_ref, x1_ref, h2_ref, a_scr):
    col = lambda j: slice(j * LANES, (j + 1) * LANES)
    for j in range(N_RET):
        a_scr[:, col(j)] = ret_ref[j].astype(BF16)
    att = [att_ref[j] for j in range(N_ATT)]
    inv = lax.rsqrt(sum(jnp.sum(t * t, axis=-1, keepdims=True) for t in att) / (N_ATT * LANES) + EPS)
    for j in range(N_ATT):
        a_scr[:, col(N_RET + j)] = (att[j] * inv * attw_ref[j]).astype(BF16)
    for j in range(N_GMLP):
        a_scr[:, col(N_RET + N_ATT + j)] = gm_ref[j].astype(BF16)
    x1 = x_ref[...] + jnp.dot(a_scr[...], w_ref[...], preferred_element_type=F32)
    x1_ref[...] = x1
    h2_ref[...] = _rms(x1, n2w_ref[...]).astype(BF16)


def _outproj(ret, att, gm, x, att_w, w, n2w, tm=256):
    S, D = x.shape
    K = w.shape[0]
    rows = lambda n: pl.BlockSpec((n, tm, LANES), lambda i: (0, i, 0))
    full = lambda: pl.BlockSpec((tm, D), lambda i: (i, 0))
    const = lambda shape: pl.BlockSpec(shape, lambda i: (0,) * len(shape))
    return pl.pallas_call(
        _outproj_kernel,
        grid=(S // tm,),
        in_specs=[rows(N_RET), rows(N_ATT), rows(N_GMLP), full(), const((N_ATT, 1, LANES)),
                  const((K, D)), const((1, D))],
        out_specs=[full(), full()],
        out_shape=[jax.ShapeDtypeStruct((S, D), F32), jax.ShapeDtypeStruct((S, D), BF16)],
        scratch_shapes=[pltpu.VMEM((tm, K), BF16)],
        compiler_params=_params("parallel"),
        name="outproj",
    )(ret, att, gm, x, att_w.reshape(N_ATT, 1, LANES), w, n2w.reshape(1, D))


def _upproj_kernel(a_ref, wg_ref, wv_ref, cwg_ref, cwv_ref, cbg_ref, cbv_ref, o_ref, hg, hv):
    tm = a_ref.shape[0]
    a = a_ref[...]

    def conv(w_ref, cw_ref, cb_ref, hbuf):
        @pl.when(pl.program_id(1) == 0)
        def _():
            hbuf[0:SUBLANES, :] = jnp.zeros((SUBLANES, hbuf.shape[1]), F32)

        h = jnp.dot(a, w_ref[...], preferred_element_type=F32)
        hbuf[SUBLANES:SUBLANES + tm, :] = h
        acc = (cb_ref[...] + cw_ref[0:1, :] * hbuf[SUBLANES - 2:SUBLANES - 2 + tm, :]
               + cw_ref[1:2, :] * hbuf[SUBLANES - 1:SUBLANES - 1 + tm, :] + cw_ref[2:3, :] * h)
        hbuf[0:SUBLANES, :] = hbuf[tm:tm + SUBLANES, :]
        return acc

    gate = conv(wg_ref, cwg_ref, cbg_ref, hg)
    val = conv(wv_ref, cwv_ref, cbv_ref, hv)
    o_ref[...] = (gate * jax.nn.sigmoid(gate) * val).astype(o_ref.dtype)


def _upproj(h2, w_up, conv_w, conv_b, tm=1024, tn=512):
    S, D = h2.shape
    F = w_up.shape[1] // 2
    nj = F // tn
    taps = conv_w.shape[0]
    return pl.pallas_call(
        _upproj_kernel,
        grid=(nj, S // tm),
        in_specs=[pl.BlockSpec((tm, D), lambda j, i: (i, 0)),
                  pl.BlockSpec((D, tn), lambda j, i: (0, j)),
                  pl.BlockSpec((D, tn), lambda j, i: (0, j + nj)),
                  pl.BlockSpec((taps, tn), lambda j, i: (0, j)),
                  pl.BlockSpec((taps, tn), lambda j, i: (0, j + nj)),
                  pl.BlockSpec((1, tn), lambda j, i: (0, j)),
                  pl.BlockSpec((1, tn), lambda j, i: (0, j + nj))],
        out_specs=pl.BlockSpec((tm, tn), lambda j, i: (i, j)),
        out_shape=jax.ShapeDtypeStruct((S, F), BF16),
        scratch_shapes=[pltpu.VMEM((tm + SUBLANES, tn), F32), pltpu.VMEM((tm + SUBLANES, tn), F32)],
        compiler_params=_params("parallel", "arbitrary"),
        name="upproj",
    )(h2, w_up, w_up, conv_w, conv_w, conv_b.reshape(1, -1), conv_b.reshape(1, -1))


def _downproj_kernel(a_ref, w_ref, x1_ref, nw_ref, *refs, emit_x):
    if emit_x:
        x2_ref, h_ref, acc_ref = refs
    else:
        h_ref, acc_ref = refs
    k = pl.program_id(1)

    @pl.when(k == 0)
    def _():
        acc_ref[...] = x1_ref[...]

    acc_ref[...] += jnp.dot(a_ref[...], w_ref[...], preferred_element_type=F32)

    @pl.when(k == pl.num_programs(1) - 1)
    def _():
        x2 = acc_ref[...]
        if emit_x:
            x2_ref[...] = x2
        h_ref[...] = _rms(x2, nw_ref[...]).astype(h_ref.dtype)


def _downproj(g, w, x1, next_w, emit_x, tm=512, tk=1408):
    S, F = g.shape
    D = w.shape[1]
    full = lambda: pl.BlockSpec((tm, D), lambda i, k: (i, 0))
    h_shape = jax.ShapeDtypeStruct((S, D), BF16 if emit_x else F32)
    out_shape = [jax.ShapeDtypeStruct((S, D), F32), h_shape] if emit_x else [h_shape]
    return pl.pallas_call(
        functools.partial(_downproj_kernel, emit_x=emit_x),
        grid=(S // tm, F // tk),
        in_specs=[pl.BlockSpec((tm, tk), lambda i, k: (i, k)), pl.BlockSpec((tk, D), lambda i, k: (k, 0)),
                  full(), pl.BlockSpec((1, D), lambda i, k: (0, 0))],
        out_specs=[full(), full()] if emit_x else [full()],
        out_shape=out_shape,
        scratch_shapes=[pltpu.VMEM((tm, D), F32)],
        compiler_params=_params("parallel", "arbitrary"),
        name="downproj",
    )(g, w, x1, next_w.reshape(1, D))


def kernel(x, norm1_w, w_in, ret_norm_w, att_norm_w, gmlp_ln_w, gmlp_ws, gmlp_bs, gmlp_out_w, w_out, norm2_w, w_up, conv_w, conv_b, w_down, final_norm_w):
    B, S, D = x.shape
    depth = w_in.shape[0]
    assert B == 1 and S % SPAN == 0 and w_in.shape[2] == N_SLABS * LANES
    tables = _retention_tables(S)
    xs = x.reshape(S, D)
    h = _rmsnorm(xs, norm1_w[0])
    for l in range(depth):
        proj = _inproj(h, w_in[l].astype(BF16))
        ret = _retention(proj, tables, ret_norm_w[l])
        att = _dilated_attention(proj)
        gm = _gmlp(proj, gmlp_ln_w[l], gmlp_ws[l], gmlp_bs[l], gmlp_out_w[l])
        x1, h2 = _outproj(ret, att, gm, xs, att_norm_w[l], w_out[l].astype(BF16), norm2_w[l])
        g = _upproj(h2, w_up[l].astype(BF16), conv_w[l], conv_b[l])
        if l + 1 < depth:
            xs, h = _downproj(g, w_down[l].astype(BF16), x1, norm1_w[l + 1], emit_x=True)
        else:
            (out,) = _downproj(g, w_down[l].astype(BF16), x1, final_norm_w, emit_x=False)
    return out.reshape(B, S, D)
```

```python
import functools

import jax
import jax.numpy as jnp
from jax import lax
from jax.experimental import pallas as pl
from jax.experimental.pallas import tpu as pltpu

F32 = jnp.float32
BF16 = jnp.bfloat16

LANES = 128
SUBLANES = 8
CHUNK = 128
N_RET = 6
N_ATT = 6
N_GMLP = 4
DILATIONS = (1, 4, 16)
SPAN = CHUNK * DILATIONS[-1]
ROPE_BASE = 10000.0
EPS = 1e-6
NEG_INF = -1e30
VMEM_LIMIT = 56 * 1024 * 1024

RQ, RK, RV, RG = 0, N_RET, 2 * N_RET, 3 * N_RET
AQ = 4 * N_RET
AK = AQ + N_ATT
AV = AK + N_ATT
GU = AV + N_ATT
GV = GU + N_GMLP
N_SLABS = GV + N_GMLP


def _params(*sem):
    return pltpu.CompilerParams(dimension_semantics=sem, vmem_limit_bytes=VMEM_LIMIT)


def _rms(x, w):
    return x * lax.rsqrt(jnp.mean(x * x, axis=-1, keepdims=True) + EPS) * w


def _rmsnorm_kernel(x_ref, w_ref, o_ref):
    o_ref[...] = _rms(x_ref[...], w_ref[...]).astype(o_ref.dtype)


def _rmsnorm(x, w, tm=512):
    S, D = x.shape
    return pl.pallas_call(
        _rmsnorm_kernel,
        grid=(S // tm,),
        in_specs=[pl.BlockSpec((tm, D), lambda i: (i, 0)), pl.BlockSpec((1, D), lambda i: (0, 0))],
        out_specs=pl.BlockSpec((tm, D), lambda i: (i, 0)),
        out_shape=jax.ShapeDtypeStruct((S, D), BF16),
        compiler_params=_params("parallel"),
        name="rmsnorm",
    )(x, w.reshape(1, D))


def _inproj_kernel(a_ref, w_ref, o_ref, wb):
    @pl.when(pl.program_id(1) == 0)
    def _():
        wb[...] = w_ref[...].astype(BF16)

    acc = jnp.dot(a_ref[...], wb[...], preferred_element_type=F32)
    for j in range(o_ref.shape[0]):
        o_ref[j] = acc[:, j * LANES:(j + 1) * LANES]


def _inproj(h, w, l, tm=1024, tn=1280):
    S, D = h.shape
    N = w.shape[2]
    return pl.pallas_call(
        _inproj_kernel,
        grid=(N // tn, S // tm),
        in_specs=[pl.BlockSpec((tm, D), lambda j, i: (i, 0)), pl.BlockSpec((None, D, tn), lambda j, i: (l, 0, j))],
        out_specs=pl.BlockSpec((tn // LANES, tm, LANES), lambda j, i: (j, i, 0)),
        out_shape=jax.ShapeDtypeStruct((N // LANES, S, LANES), F32),
        scratch_shapes=[pltpu.VMEM((D, tn), BF16)],
        compiler_params=_params("parallel", "arbitrary"),
        name="inproj",
    )(h, w)


def _retention_kernel(q_ref, k_ref, v_ref, g_ref, cos_ref, sin_ref, dec_ref, xi_ref, zeta_ref, cd_ref,
                      nw_ref, o_ref, state_ref):
    @pl.when(pl.program_id(1) == 0)
    def _():
        state_ref[...] = jnp.zeros_like(state_ref)

    dec, xi, zeta, cd, nw = dec_ref[...], xi_ref[...], zeta_ref[...], cd_ref[...], nw_ref[...]
    scale = LANES ** -0.5
    for c in range(q_ref.shape[0] // CHUNK):
        rows = pl.ds(c * CHUNK, CHUNK)
        cos, sin = cos_ref[rows, :], sin_ref[rows, :]
        q, k = q_ref[rows, :], k_ref[rows, :]
        qr = q * cos + pltpu.roll(q, LANES // 2, 1) * sin
        kr = (k * cos + pltpu.roll(k, LANES // 2, 1) * sin) * scale
        vb = v_ref[rows, :].astype(BF16)
        s = lax.dot_general(qr.astype(BF16), kr.astype(BF16), (((1,), (1,)), ((), ())),
                            preferred_element_type=F32) * dec
        inner = jnp.dot(s.astype(BF16), vb, preferred_element_type=F32)
        state = state_ref[...]
        cross = jnp.dot((qr * xi).astype(BF16), state.astype(BF16), preferred_element_type=F32)
        upd = lax.dot_general((kr * zeta).astype(BF16), vb, (((0,), (0,)), ((), ())),
                              preferred_element_type=F32)
        state_ref[...] = cd * state + upd
        g = g_ref[rows, :]
        o_ref[rows, :] = (g * jax.nn.sigmoid(g)) * _rms(inner + cross, nw)


def _retention_tables(S):
    half = LANES // 2
    inv_freq = ROPE_BASE ** (-jnp.arange(half, dtype=F32) / half)
    ang = jnp.arange(S, dtype=jnp.int32).astype(F32)[:, None] * inv_freq[None, :]
    cos, sin = jnp.cos(ang), jnp.sin(ang)
    cosf = jnp.concatenate([cos, cos], axis=-1)
    sinf = jnp.concatenate([-sin, sin], axis=-1)
    log_g = jnp.log1p(-jnp.exp2(-5.0 - jnp.arange(N_RET, dtype=F32)))
    idx = jnp.arange(CHUNK, dtype=F32)
    diff = idx[:, None] - idx[None, :]
    dec = jnp.where(diff[None] >= 0, jnp.exp(jnp.maximum(diff, 0.0)[None] * log_g[:, None, None]), 0.0)
    xi = jnp.exp((idx[None, :] + 1.0) * log_g[:, None])
    zeta = jnp.exp((CHUNK - 1.0 - idx)[None, :] * log_g[:, None])
    cd = jnp.exp(CHUNK * log_g)
    bcast = lambda t: jnp.broadcast_to(t[:, :, None], (N_RET, CHUNK, LANES))
    return cosf, sinf, dec, bcast(xi), bcast(zeta), jnp.broadcast_to(cd[:, None, None], (N_RET, 1, LANES))


def _retention(proj, tables, norm_w, tr=512):
    S = proj.shape[1]
    cosf, sinf, dec, xi, zeta, cd = tables
    slab = lambda off: pl.BlockSpec((None, tr, LANES), lambda h, i: (off + h, i, 0))
    tab = pl.BlockSpec((tr, LANES), lambda h, i: (i, 0))
    per_head = lambda r: pl.BlockSpec((None, r, LANES), lambda h, i: (h, 0, 0))
    return pl.pallas_call(
        _retention_kernel,
        grid=(N_RET, S // tr),
        in_specs=[slab(RQ), slab(RK), slab(RV), slab(RG), tab, tab,
                  per_head(CHUNK), per_head(CHUNK), per_head(CHUNK), per_head(1), per_head(1)],
        out_specs=pl.BlockSpec((None, tr, LANES), lambda h, i: (h, i, 0)),
        out_shape=jax.ShapeDtypeStruct((N_RET, S, LANES), F32),
        scratch_shapes=[pltpu.VMEM((LANES, LANES), F32)],
        compiler_params=_params("parallel", "arbitrary"),
        name="retention",
    )(proj, proj, proj, proj, cosf, sinf, dec, xi, zeta, cd, norm_w.reshape(N_RET, 1, LANES))


def _rows(start, size, stride):
    return pl.ds(start, size) if stride == 1 else pl.ds(start, size, stride=stride)


def _dilated_kernel(q_ref, k_ref, v_ref, o_ref, kk, vv, o1, o2, o3, l1, l2, l3):
    n = pl.program_id(1)

    @pl.when(n == 0)
    def _():
        kk[0:SPAN, :] = jnp.zeros((SPAN, LANES), F32)
        vv[0:SPAN, :] = jnp.zeros((SPAN, LANES), F32)

    @pl.when(n > 0)
    def _():
        kk[0:SPAN, :] = kk[SPAN:2 * SPAN, :]
        vv[0:SPAN, :] = vv[SPAN:2 * SPAN, :]

    kk[SPAN:2 * SPAN, :] = k_ref[...]
    vv[SPAN:2 * SPAN, :] = v_ref[...]

    a = lax.broadcasted_iota(jnp.int32, (CHUNK, 2 * CHUNK), 0)
    c = lax.broadcasted_iota(jnp.int32, (CHUNK, 2 * CHUNK), 1)
    band = (c >= a) & (c <= a + CHUNK)
    in_block = c >= CHUNK
    scale = LANES ** -0.5

    def unit(qstart, kstart, stride, has_prev):
        q = q_ref[_rows(qstart, CHUNK, stride), :]
        k = kk[_rows(kstart, 2 * CHUNK, stride), :]
        v = vv[_rows(kstart, 2 * CHUNK, stride), :]
        s = lax.dot_general(q.astype(BF16), k.astype(BF16), (((1,), (1,)), ((), ())),
                            preferred_element_type=F32) * scale
        s = jnp.where(band & (in_block | has_prev), s, NEG_INF)
        m = jnp.max(s, axis=-1, keepdims=True)
        p = jnp.exp(s - m)
        den = jnp.sum(p, axis=-1, keepdims=True)
        o = jnp.dot(p.astype(BF16), v.astype(BF16), preferred_element_type=F32) / den
        return o, jnp.broadcast_to(m + jnp.log(den), (CHUNK, LANES))

    d1, d2, d3 = DILATIONS
    sub2 = CHUNK * d2
    per2 = SPAN // sub2

    def body(it, carry):
        r1 = unit(it * CHUNK, SPAN + (it - 1) * CHUNK, d1, (n > 0) | (it > 0))
        ss, r = it // d2, it % d2
        r2 = unit(ss * sub2 + r, SPAN + (ss - 1) * sub2 + r, d2, (n > 0) | (ss > 0))
        r3 = unit(it, it, d3, n > 0)
        for (o, lse), o_s, l_s, start, stride in ((r1, o1, l1, it * CHUNK, d1), (r2, o2, l2, ss * sub2 + r, d2),
                                                 (r3, o3, l3, it, d3)):
            o_s[_rows(start, CHUNK, stride), :] = o
            l_s[_rows(start, CHUNK, stride), :] = lse
        return carry

    assert per2 * d2 == d3 == SPAN // CHUNK
    lax.fori_loop(0, d3, body, 0)

    def merge(b, carry):
        rows = pl.ds(b * CHUNK, CHUNK)
        a1, a2, a3 = l1[rows, :], l2[rows, :], l3[rows, :]
        mx = jnp.maximum(jnp.maximum(a1, a2), a3)
        w1, w2, w3 = jnp.exp(a1 - mx), jnp.exp(a2 - mx), jnp.exp(a3 - mx)
        o_ref[rows, :] = (w1 * o1[rows, :] + w2 * o2[rows, :] + w3 * o3[rows, :]) / (w1 + w2 + w3)
        return carry

    lax.fori_loop(0, SPAN // CHUNK, merge, 0)


def _dilated_attention(proj):
    S = proj.shape[1]
    slab = lambda off: pl.BlockSpec((None, SPAN, LANES), lambda h, n: (off + h, n, 0))
    big = pltpu.VMEM((2 * SPAN, LANES), F32)
    acc = pltpu.VMEM((SPAN, LANES), F32)
    return pl.pallas_call(
        _dilated_kernel,
        grid=(N_ATT, S // SPAN),
        in_specs=[slab(AQ), slab(AK), slab(AV)],
        out_specs=pl.BlockSpec((None, SPAN, LANES), lambda h, n: (h, n, 0)),
        out_shape=jax.ShapeDtypeStruct((N_ATT, S, LANES), F32),
        scratch_shapes=[big, big, acc, acc, acc, acc, acc, acc],
        compiler_params=_params("parallel", "arbitrary"),
        name="dilated_attention",
    )(proj, proj, proj)


def _gmlp_kernel(*refs):
    G = N_GMLP
    u_refs, v_refs = refs[:G], refs[G:2 * G]
    lnw_ref, ws_ref, bs_ref, ow_ref, o_ref = refs[2 * G:]
    tm = o_ref.shape[1]
    width = G * LANES
    gv = [jax.nn.gelu(r[...]) for r in v_refs]
    mu = sum(jnp.sum(t, axis=-1, keepdims=True) for t in gv) / width
    xc = [t - mu for t in gv]
    inv = lax.rsqrt(sum(jnp.sum(t * t, axis=-1, keepdims=True) for t in xc) / width + EPS)
    qi = lax.broadcasted_iota(jnp.int32, (CHUNK, CHUNK), 0)
    ki = lax.broadcasted_iota(jnp.int32, (CHUNK, CHUNK), 1)
    outs = []
    for g in range(G):
        vn = (xc[g] * inv * lnw_ref[g]).astype(BF16)
        wm = jnp.where(ki <= qi, ws_ref[g], 0.0).astype(BF16)
        sp = [jnp.dot(wm, vn[c * CHUNK:(c + 1) * CHUNK, :], preferred_element_type=F32) + bs_ref[g]
              for c in range(tm // CHUNK)]
        outs.append(jax.nn.gelu(u_refs[g][...]) * jnp.concatenate(sp, axis=0))
    inv_o = lax.rsqrt(sum(jnp.sum(t * t, axis=-1, keepdims=True) for t in outs) / width + EPS)
    for g in range(G):
        o_ref[g] = outs[g] * inv_o * ow_ref[g]


def _gmlp(proj, ln_w, ws, bs, out_w, tm=512):
    S = proj.shape[1]
    G = N_GMLP
    slab = lambda s: pl.BlockSpec((None, tm, LANES), lambda i: (s, i, 0))
    small = lambda shape: pl.BlockSpec(shape, lambda i: (0,) * len(shape))
    return pl.pallas_call(
        _gmlp_kernel,
        grid=(S // tm,),
        in_specs=[slab(GU + g) for g in range(G)] + [slab(GV + g) for g in range(G)]
        + [small((G, 1, LANES)), small((G, CHUNK, CHUNK)), small((G, CHUNK, 1)), small((G, 1, LANES))],
        out_specs=pl.BlockSpec((G, tm, LANES), lambda i: (0, i, 0)),
        out_shape=jax.ShapeDtypeStruct((G, S, LANES), F32),
        compiler_params=_params("parallel"),
        name="gmlp",
    )(*([proj] * (2 * G)), ln_w.reshape(G, 1, LANES), ws, bs.reshape(G, CHUNK, 1), out_w.reshape(G, 1, LANES))


def _outproj_kernel(ret_ref, att_ref, gm_ref, x_ref, attw_ref, w_ref, n2w_ref, x1_ref, h2_ref, a_scr, wb, *, nc):
    @pl.when(pl.program_id(0) == 0)
    def _():
        wb[...] = w_ref[...].astype(BF16)

    col = lambda j: slice(j * LANES, (j + 1) * LANES)
    for j in range(N_RET):
        a_scr[:, col(j)] = ret_ref[j].astype(BF16)
    att = [att_ref[j] for j in range(N_ATT)]
    inv = lax.rsqrt(sum(jnp.sum(t * t, axis=-1, keepdims=True) for t in att) / (N_ATT * LANES) + EPS)
    for j in range(N_ATT):
        a_scr[:, col(N_RET + j)] = (att[j] * inv * attw_ref[j]).astype(BF16)
    for j in range(N_GMLP):
        a_scr[:, col(N_RET + N_ATT + j)] = gm_ref[j].astype(BF16)
    D = x_ref.shape[1]
    ssq = 0.0
    for c in range(D // nc):
        cs = slice(c * nc, (c + 1) * nc)
        x1 = x_ref[:, cs] + jnp.dot(a_scr[...], wb[:, cs], preferred_element_type=F32)
        x1_ref[:, cs] = x1
        ssq = ssq + jnp.sum(x1 * x1, axis=-1, keepdims=True)
    h2_ref[...] = (x1_ref[...] * lax.rsqrt(ssq / D + EPS) * n2w_ref[...]).astype(BF16)


def _outproj(ret, att, gm, x, att_w, w, l, n2w, tm=256, nc=512):
    S, D = x.shape
    K = w.shape[1]
    rows = lambda n: pl.BlockSpec((n, tm, LANES), lambda i: (0, i, 0))
    full = lambda: pl.BlockSpec((tm, D), lambda i: (i, 0))
    const = lambda shape: pl.BlockSpec(shape, lambda i: (0,) * len(shape))
    return pl.pallas_call(
        functools.partial(_outproj_kernel, nc=nc),
        grid=(S // tm,),
        in_specs=[rows(N_RET), rows(N_ATT), rows(N_GMLP), full(), const((N_ATT, 1, LANES)),
                  pl.BlockSpec((None, K, D), lambda i: (l, 0, 0), pipeline_mode=pl.Buffered(1)), const((1, D))],
        out_specs=[full(), full()],
        out_shape=[jax.ShapeDtypeStruct((S, D), F32), jax.ShapeDtypeStruct((S, D), BF16)],
        scratch_shapes=[pltpu.VMEM((tm, K), BF16), pltpu.VMEM((K, D), BF16)],
        compiler_params=_params("arbitrary"),
        name="outproj",
    )(ret, att, gm, x, att_w.reshape(N_ATT, 1, LANES), w, n2w.reshape(1, D))


def _upproj_kernel(a_ref, wg_ref, wv_ref, cwg_ref, cwv_ref, cbg_ref, cbv_ref, o_ref, wgb, wvb, hg, hv):
    tm = a_ref.shape[0]
    nslab = hg.shape[0]
    head = SUBLANES

    @pl.when(pl.program_id(1) == 0)
    def _():
        wgb[...] = wg_ref[...].astype(BF16)
        wvb[...] = wv_ref[...].astype(BF16)
        hg[:, 0:head, :] = jnp.zeros((nslab, head, LANES), F32)
        hv[:, 0:head, :] = jnp.zeros((nslab, head, LANES), F32)

    a = a_ref[...]

    def conv(wb, cw_ref, cb_ref, hbuf):
        h = jnp.dot(a, wb[...], preferred_element_type=F32)
        outs = []
        for s in range(nslab):
            cs = slice(s * LANES, (s + 1) * LANES)
            hs = h[:, cs]
            hbuf[s, head:head + tm, :] = hs
            outs.append(cb_ref[:, cs] + cw_ref[0:1, cs] * hbuf[s, head - 2:head - 2 + tm, :]
                        + cw_ref[1:2, cs] * hbuf[s, head - 1:head - 1 + tm, :] + cw_ref[2:3, cs] * hs)
            hbuf[s, 0:head, :] = hbuf[s, tm:tm + head, :]
        return outs

    gate = conv(wgb, cwg_ref, cbg_ref, hg)
    val = conv(wvb, cwv_ref, cbv_ref, hv)
    for s in range(nslab):
        o_ref[:, s * LANES:(s + 1) * LANES] = (gate[s] * jax.nn.sigmoid(gate[s]) * val[s]).astype(o_ref.dtype)


def _upproj(h2, w_up, conv_w, conv_b, l, tm=1024, tn=512):
    S, D = h2.shape
    F = w_up.shape[2] // 2
    nj = F // tn
    taps = conv_w.shape[1]
    hbuf = pltpu.VMEM((tn // LANES, tm + SUBLANES, LANES), F32)
    return pl.pallas_call(
        _upproj_kernel,
        grid=(nj, S // tm),
        in_specs=[pl.BlockSpec((tm, D), lambda j, i: (i, 0)),
                  pl.BlockSpec((None, D, tn), lambda j, i: (l, 0, j)),
                  pl.BlockSpec((None, D, tn), lambda j, i: (l, 0, j + nj)),
                  pl.BlockSpec((None, taps, tn), lambda j, i: (l, 0, j)),
                  pl.BlockSpec((None, taps, tn), lambda j, i: (l, 0, j + nj)),
                  pl.BlockSpec((None, 1, tn), lambda j, i: (l, 0, j)),
                  pl.BlockSpec((None, 1, tn), lambda j, i: (l, 0, j + nj))],
        out_specs=pl.BlockSpec((tm, tn), lambda j, i: (i, j)),
        out_shape=jax.ShapeDtypeStruct((S, F), BF16),
        scratch_shapes=[pltpu.VMEM((D, tn), BF16), pltpu.VMEM((D, tn), BF16), hbuf, hbuf],
        compiler_params=_params("parallel", "arbitrary"),
        name="upproj",
    )(h2, w_up, w_up, conv_w, conv_w, conv_b[:, None, :], conv_b[:, None, :])


def _downproj_kernel(a_ref, w_ref, x1_ref, nw_ref, *refs, emit_x, nc):
    if emit_x:
        x2_ref, h_ref, acc_ref = refs
    else:
        h_ref, acc_ref = refs
    k = pl.program_id(1)

    @pl.when(k == 0)
    def _():
        acc_ref[...] = x1_ref[...]

    a = a_ref[...]
    for c in range(acc_ref.shape[1] // nc):
        cs = slice(c * nc, (c + 1) * nc)
        acc_ref[:, cs] += jnp.dot(a, w_ref[:, cs], preferred_element_type=F32)

    @pl.when(k == pl.num_programs(1) - 1)
    def _():
        x2 = acc_ref[...]
        if emit_x:
            x2_ref[...] = x2
        h_ref[...] = _rms(x2, nw_ref[...]).astype(h_ref.dtype)


def _downproj(g, w, l, x1, next_w, emit_x, tm=512, tk=1408, nc=512):
    S, F = g.shape
    D = w.shape[2]
    full = lambda: pl.BlockSpec((tm, D), lambda i, k: (i, 0))
    h_shape = jax.ShapeDtypeStruct((S, D), BF16 if emit_x else F32)
    out_shape = [jax.ShapeDtypeStruct((S, D), F32), h_shape] if emit_x else [h_shape]
    return pl.pallas_call(
        functools.partial(_downproj_kernel, emit_x=emit_x, nc=nc),
        grid=(S // tm, F // tk),
        in_specs=[pl.BlockSpec((tm, tk), lambda i, k: (i, k)), pl.BlockSpec((None, tk, D), lambda i, k: (l, k, 0)),
                  full(), pl.BlockSpec((1, D), lambda i, k: (0, 0))],
        out_specs=[full(), full()] if emit_x else [full()],
        out_shape=out_shape,
        scratch_shapes=[pltpu.VMEM((tm, D), F32)],
        compiler_params=_params("parallel", "arbitrary"),
        name="downproj",
    )(g, w, x1, next_w.reshape(1, D))


def kernel(x, norm1_w, w_in, ret_norm_w, att_norm_w, gmlp_ln_w, gmlp_ws, gmlp_bs, gmlp_out_w, w_out, norm2_w, w_up, conv_w, conv_b, w_down, final_norm_w):
    B, S, D = x.shape
    depth = w_in.shape[0]
    assert B == 1 and S % SPAN == 0 and w_in.shape[2] == N_SLABS * LANES
    tables = _retention_tables(S)
    xs = x.reshape(S, D)
    h = _rmsnorm(xs, norm1_w[0])
    w_down_b = w_down.astype(BF16)
    for l in range(depth):
        proj = _inproj(h, w_in, l)
        ret = _retention(proj, tables, ret_norm_w[l])
        att = _dilated_attention(proj)
        gm = _gmlp(proj, gmlp_ln_w[l], gmlp_ws[l], gmlp_bs[l], gmlp_out_w[l])
        x1, h2 = _outproj(ret, att, gm, xs, att_norm_w[l], w_out, l, norm2_w[l])
        g = _upproj(h2, w_up, conv_w, conv_b, l)
        if l + 1 < depth:
            xs, h = _downproj(g, w_down_b, l, x1, norm1_w[l + 1], emit_x=True)
        else:
            (out,) = _downproj(g, w_down_b, l, x1, final_norm_w, emit_x=False)
    return out.reshape(B, S, D)
```

```python
import functools

import jax
import jax.numpy as jnp
from jax import lax
from jax.experimental import pallas as pl
from jax.experimental.pallas import tpu as pltpu

F32 = jnp.float32
BF16 = jnp.bfloat16

LANES = 128
SUBLANES = 8
CHUNK = 128
N_RET = 6
N_ATT = 6
N_GMLP = 4
DILATIONS = (1, 4, 16)
SPAN = CHUNK * DILATIONS[-1]
ROPE_BASE = 10000.0
EPS = 1e-6
NEG_INF = -1e30
VMEM_LIMIT = 56 * 1024 * 1024

RQ, RK, RV, RG = 0, N_RET, 2 * N_RET, 3 * N_RET
AQ = 4 * N_RET
AK = AQ + N_ATT
AV = AK + N_ATT
GU = AV + N_ATT
GV = GU + N_GMLP
N_SLABS = GV + N_GMLP


def _params(*sem):
    return pltpu.CompilerParams(dimension_semantics=sem, vmem_limit_bytes=VMEM_LIMIT)


def _rms(x, w):
    return x * lax.rsqrt(jnp.mean(x * x, axis=-1, keepdims=True) + EPS) * w


def _silu(x):
    hx = 0.5 * x
    return hx + hx * jnp.tanh(hx)


def _rmsnorm_kernel(x_ref, w_ref, o_ref):
    o_ref[...] = _rms(x_ref[...], w_ref[...]).astype(o_ref.dtype)


def _rmsnorm(x, w, tm=512):
    S, D = x.shape
    return pl.pallas_call(
        _rmsnorm_kernel,
        grid=(S // tm,),
        in_specs=[pl.BlockSpec((tm, D), lambda i: (i, 0)), pl.BlockSpec((1, D), lambda i: (0, 0))],
        out_specs=pl.BlockSpec((tm, D), lambda i: (i, 0)),
        out_shape=jax.ShapeDtypeStruct((S, D), x.dtype),
        compiler_params=_params("parallel"),
        name="rmsnorm",
    )(x, w.reshape(1, D))


def _inproj_kernel(x_ref, nw_ref, w_ref, o_ref, wb):
    @pl.when(pl.program_id(1) == 0)
    def _():
        wb[...] = w_ref[...].astype(BF16)

    a = _rms(x_ref[...], nw_ref[...]).astype(BF16)
    acc = jnp.dot(a, wb[...], preferred_element_type=F32)
    for j in range(o_ref.shape[0]):
        o_ref[j] = acc[:, j * LANES:(j + 1) * LANES]


def _inproj(x, nw, w, l, tm=1024, tn=1280):
    S, D = x.shape
    N = w.shape[2]
    return pl.pallas_call(
        _inproj_kernel,
        grid=(N // tn, S // tm),
        in_specs=[pl.BlockSpec((tm, D), lambda j, i: (i, 0)), pl.BlockSpec((1, D), lambda j, i: (0, 0)),
                  pl.BlockSpec((None, D, tn), lambda j, i: (l, 0, j), pipeline_mode=pl.Buffered(1))],
        out_specs=pl.BlockSpec((tn // LANES, tm, LANES), lambda j, i: (j, i, 0)),
        out_shape=jax.ShapeDtypeStruct((N // LANES, S, LANES), F32),
        scratch_shapes=[pltpu.VMEM((D, tn), BF16)],
        compiler_params=_params("parallel", "arbitrary"),
        name="inproj",
    )(x, nw.reshape(1, D), w)


def _retention_kernel(q_ref, k_ref, v_ref, g_ref, cos_ref, sin_ref, dec_ref, xi_ref, zeta_ref, cd_ref,
                      nw_ref, o_ref, state_ref):
    @pl.when(pl.program_id(1) == 0)
    def _():
        state_ref[...] = jnp.zeros_like(state_ref)

    scale = LANES ** -0.5
    nh, tr = q_ref.shape[0], q_ref.shape[1]
    for c in range(tr // CHUNK):
        rows = pl.ds(c * CHUNK, CHUNK)
        cos, sin = cos_ref[rows, :], sin_ref[rows, :]
        for h in range(nh):
            q, k = q_ref[h, rows, :], k_ref[h, rows, :]
            qr = q * cos + pltpu.roll(q, LANES // 2, 1) * sin
            kr = (k * cos + pltpu.roll(k, LANES // 2, 1) * sin) * scale
            vb = v_ref[h, rows, :].astype(BF16)
            s = lax.dot_general(qr.astype(BF16), kr.astype(BF16), (((1,), (1,)), ((), ())),
                                preferred_element_type=F32) * dec_ref[h]
            state = state_ref[h]
            lhs = jnp.concatenate([s.astype(BF16), (qr * xi_ref[h]).astype(BF16)], axis=1)
            rhs = jnp.concatenate([vb, state.astype(BF16)], axis=0)
            o = jnp.dot(lhs, rhs, preferred_element_type=F32)
            upd = lax.dot_general((kr * zeta_ref[h]).astype(BF16), vb, (((0,), (0,)), ((), ())),
                                  preferred_element_type=F32)
            state_ref[h] = cd_ref[h] * state + upd
            o_ref[h, rows, :] = _silu(g_ref[h, rows, :]) * _rms(o, nw_ref[h])


def _retention_tables(S):
    half = LANES // 2
    inv_freq = ROPE_BASE ** (-jnp.arange(half, dtype=F32) / half)
    ang = jnp.arange(S, dtype=jnp.int32).astype(F32)[:, None] * inv_freq[None, :]
    cos, sin = jnp.cos(ang), jnp.sin(ang)
    cosf = jnp.concatenate([cos, cos], axis=-1)
    sinf = jnp.concatenate([-sin, sin], axis=-1)
    log_g = jnp.log1p(-jnp.exp2(-5.0 - jnp.arange(N_RET, dtype=F32)))
    idx = jnp.arange(CHUNK, dtype=F32)
    diff = idx[:, None] - idx[None, :]
    dec = jnp.where(diff[None] >= 0, jnp.exp(jnp.maximum(diff, 0.0)[None] * log_g[:, None, None]), 0.0)
    xi = jnp.exp((idx[None, :] + 1.0) * log_g[:, None])
    zeta = jnp.exp((CHUNK - 1.0 - idx)[None, :] * log_g[:, None])
    cd = jnp.exp(CHUNK * log_g)
    bcast = lambda t: jnp.broadcast_to(t[:, :, None], (N_RET, CHUNK, LANES))
    return cosf, sinf, dec, bcast(xi), bcast(zeta), jnp.broadcast_to(cd[:, None, None], (N_RET, 1, LANES))


def _retention(proj, tables, norm_w, tr=256, nh=N_RET):
    S = proj.shape[1]
    cosf, sinf, dec, xi, zeta, cd = tables
    assert all(off % nh == 0 for off in (RQ, RK, RV, RG)) and N_RET % nh == 0
    slab = lambda off: pl.BlockSpec((nh, tr, LANES), lambda h, i: (off // nh + h, i, 0))
    tab = pl.BlockSpec((tr, LANES), lambda h, i: (i, 0))
    per_head = lambda r: pl.BlockSpec((nh, r, LANES), lambda h, i: (h, 0, 0))
    return pl.pallas_call(
        _retention_kernel,
        grid=(N_RET // nh, S // tr),
        in_specs=[slab(RQ), slab(RK), slab(RV), slab(RG), tab, tab,
                  per_head(CHUNK), per_head(CHUNK), per_head(CHUNK), per_head(1), per_head(1)],
        out_specs=pl.BlockSpec((nh, tr, LANES), lambda h, i: (h, i, 0)),
        out_shape=jax.ShapeDtypeStruct((N_RET, S, LANES), F32),
        scratch_shapes=[pltpu.VMEM((nh, LANES, LANES), F32)],
        compiler_params=_params("parallel", "arbitrary"),
        name="retention",
    )(proj, proj, proj, proj, cosf, sinf, dec, xi, zeta, cd, norm_w.reshape(N_RET, 1, LANES))


D2 = DILATIONS[1]
TROWS = SPAN // D2
UNITS = 3 * (SPAN // CHUNK)
assert DILATIONS == (1, D2, D2 * D2) and SPAN == CHUNK * D2 * D2


def _dilated_kernel(q_ref, k_ref, v_ref, o_ref, tq, tk, tv, kn, vn, sbuf, o1, d1, m1, o2, d2, m2, o3, d3, m3):
    n = pl.program_id(1)
    slot = n % 2
    pslot = 1 - slot
    nblk = SPAN // CHUNK
    scale = LANES ** -0.5
    strided = lambda start: pl.ds(start, CHUNK, stride=D2)
    full = (slice(None),)

    @pl.when(n == 0)
    def _():
        tk[1] = jnp.zeros(tk.shape[1:], F32)
        tv[1] = jnp.zeros(tv.shape[1:], F32)
        kn[0:CHUNK, :] = jnp.zeros((CHUNK, LANES), F32)
        vn[0:CHUNK, :] = jnp.zeros((CHUNK, LANES), F32)

    @pl.when(n > 0)
    def _():
        kn[0:CHUNK, :] = kn[SPAN:SPAN + CHUNK, :]
        vn[0:CHUNK, :] = vn[SPAN:SPAN + CHUNK, :]

    kn[CHUNK:CHUNK + SPAN, :] = k_ref[...]
    vn[CHUNK:CHUNK + SPAN, :] = v_ref[...]
    for r in range(D2):
        tq[r] = q_ref[pl.ds(r, TROWS, stride=D2), :]
        tk[slot, r] = k_ref[pl.ds(r, TROWS, stride=D2), :]
        tv[slot, r] = v_ref[pl.ds(r, TROWS, stride=D2), :]

    qi = lax.broadcasted_iota(jnp.int32, (CHUNK, CHUNK), 0)
    ci = lax.broadcasted_iota(jnp.int32, (CHUNK, CHUNK), 1)
    own_ok = ci <= qi

    def units(it):
        ss, r = it // D2, it % D2
        p2 = jnp.where(ss > 0, slot, pslot)
        blk = lambda i: pl.ds(i * CHUNK, CHUNK)
        return (
            (q_ref, (blk(it),), (blk(it),), (blk(it + 1),), (n > 0) | (it > 0), kn, vn),
            (tq, (r, blk(ss)), (p2, r, blk((ss + D2 - 1) % D2)), (slot, r, blk(ss)), (n > 0) | (ss > 0), tk, tv),
            (tq, (r, strided(ss)), (pslot, r, strided(ss)), (slot, r, strided(ss)), n > 0, tk, tv),
        )

    def out_rows(it):
        ss, r = it // D2, it % D2
        return ((pl.ds(it * CHUNK, CHUNK),), (r, pl.ds(ss * CHUNK, CHUNK)), (r, strided(ss)))

    def phase_a(it, carry):
        for p, (qr, qidx, pidx, oidx, has_prev, kr, _), m_s, rows in zip(
                range(3), units(it), (m1, m2, m3), out_rows(it)):
            q = qr[qidx + full].astype(BF16)
            k = jnp.concatenate([kr[pidx + full], kr[oidx + full]], axis=0).astype(BF16)
            s = lax.dot_general(q, k, (((1,), (1,)), ((), ())), preferred_element_type=F32)
            sp = jnp.where(ci >= qi + jnp.where(has_prev, 0, CHUNK), s[:, :CHUNK], NEG_INF)
            so = jnp.where(own_ok, s[:, CHUNK:], NEG_INF)
            sbuf[3 * it + p, :, 0:CHUNK] = sp
            sbuf[3 * it + p, :, CHUNK:2 * CHUNK] = so
            m = jnp.max(jnp.maximum(sp, so), axis=-1, keepdims=True)
            m_s[rows + full] = jnp.broadcast_to(m, (CHUNK, LANES))
        return carry

    ones = jnp.ones((2 * CHUNK, LANES), BF16)

    def phase_b(it, carry):
        for p, (_, _, pidx, oidx, _, _, vr), o_s, d_s, m_s, rows in zip(
                range(3), units(it), (o1, o2, o3), (d1, d2, d3), (m1, m2, m3), out_rows(it)):
            m = m_s[rows + full]
            e = jnp.concatenate([jnp.exp((sbuf[3 * it + p, :, 0:CHUNK] - m) * scale),
                                 jnp.exp((sbuf[3 * it + p, :, CHUNK:2 * CHUNK] - m) * scale)], axis=1)
            v = jnp.concatenate([vr[pidx + full], vr[oidx + full]], axis=0).astype(BF16)
            oa = jnp.dot(e.astype(BF16), jnp.concatenate([v, ones], axis=1), preferred_element_type=F32)
            o_s[rows + full] = oa[:, :LANES]
            d_s[rows + full] = oa[:, LANES:]
        return carry

    lax.fori_loop(0, nblk, phase_a, 0, unroll=True)
    lax.fori_loop(0, nblk, phase_b, 0, unroll=True)

    def phase_c(j, carry):
        for r in range(D2):
            nat = (pl.ds(j * CHUNK * D2 + r, CHUNK, stride=D2), slice(None))
            trw = (r, pl.ds(j * CHUNK, CHUNK), slice(None))
            ma, mb, mc = m1[nat], m2[trw], m3[trw]
            mx = jnp.maximum(jnp.maximum(ma, mb), mc)
            wa, wb, wc = (jnp.exp((t - mx) * scale) for t in (ma, mb, mc))
            num = wa * o1[nat] + wb * o2[trw] + wc * o3[trw]
            den = wa * d1[nat] + wb * d2[trw] + wc * d3[trw]
            o_ref[nat] = num / den
        return carry

    lax.fori_loop(0, TROWS // CHUNK, phase_c, 0)


def _dilated_attention(proj):
    S = proj.shape[1]
    slab = lambda off: pl.BlockSpec((None, SPAN, LANES), lambda h, n: (off + h, n, 0))
    nat = pltpu.VMEM((SPAN, LANES), F32)
    res = pltpu.VMEM((D2, TROWS, LANES), F32)
    ring = pltpu.VMEM((2, D2, TROWS, LANES), F32)
    tail = pltpu.VMEM((CHUNK + SPAN, LANES), F32)
    return pl.pallas_call(
        _dilated_kernel,
        grid=(N_ATT, S // SPAN),
        in_specs=[slab(AQ), slab(AK), slab(AV)],
        out_specs=pl.BlockSpec((None, SPAN, LANES), lambda h, n: (h, n, 0)),
        out_shape=jax.ShapeDtypeStruct((N_ATT, S, LANES), F32),
        scratch_shapes=[res, ring, ring, tail, tail, pltpu.VMEM((UNITS, CHUNK, 2 * CHUNK), F32),
                        nat, nat, nat, res, res, res, res, res, res],
        compiler_params=_params("parallel", "arbitrary"),
        name="dilated_attention",
    )(proj, proj, proj)


def _gmlp_kernel(*refs):
    G = N_GMLP
    u_refs, v_refs = refs[:G], refs[G:2 * G]
    lnw_ref, ws_ref, bs_ref, ow_ref, o_ref = refs[2 * G:]
    tm = o_ref.shape[1]
    width = G * LANES
    gv = [jax.nn.gelu(r[...]) for r in v_refs]
    mu = sum(jnp.sum(t, axis=-1, keepdims=True) for t in gv) / width
    xc = [t - mu for t in gv]
    inv = lax.rsqrt(sum(jnp.sum(t * t, axis=-1, keepdims=True) for t in xc) / width + EPS)
    qi = lax.broadcasted_iota(jnp.int32, (CHUNK, CHUNK), 0)
    ki = lax.broadcasted_iota(jnp.int32, (CHUNK, CHUNK), 1)
    outs = []
    for g in range(G):
        vn = (xc[g] * inv * lnw_ref[g]).astype(BF16)
        wm = jnp.where(ki <= qi, ws_ref[g], 0.0).astype(BF16)
        sp = [jnp.dot(wm, vn[c * CHUNK:(c + 1) * CHUNK, :], preferred_element_type=F32) + bs_ref[g]
              for c in range(tm // CHUNK)]
        outs.append(jax.nn.gelu(u_refs[g][...]) * jnp.concatenate(sp, axis=0))
    inv_o = lax.rsqrt(sum(jnp.sum(t * t, axis=-1, keepdims=True) for t in outs) / width + EPS)
    for g in range(G):
        o_ref[g] = outs[g] * inv_o * ow_ref[g]


def _gmlp(proj, ln_w, ws, bs, out_w, tm=512):
    S = proj.shape[1]
    G = N_GMLP
    slab = lambda s: pl.BlockSpec((None, tm, LANES), lambda i: (s, i, 0))
    small = lambda shape: pl.BlockSpec(shape, lambda i: (0,) * len(shape))
    return pl.pallas_call(
        _gmlp_kernel,
        grid=(S // tm,),
        in_specs=[slab(GU + g) for g in range(G)] + [slab(GV + g) for g in range(G)]
        + [small((G, 1, LANES)), small((G, CHUNK, CHUNK)), small((G, CHUNK, 1)), small((G, 1, LANES))],
        out_specs=pl.BlockSpec((G, tm, LANES), lambda i: (0, i, 0)),
        out_shape=jax.ShapeDtypeStruct((G, S, LANES), F32),
        compiler_params=_params("parallel"),
        name="gmlp",
    )(*([proj] * (2 * G)), ln_w.reshape(G, 1, LANES), ws, bs.reshape(G, CHUNK, 1), out_w.reshape(G, 1, LANES))


def _outproj_kernel(ret_ref, att_ref, gm_ref, x_ref, attw_ref, wb, n2w_ref, x1_ref, h2_ref, a_scr, *, nc):
    col = lambda j: slice(j * LANES, (j + 1) * LANES)
    for j in range(N_RET):
        a_scr[:, col(j)] = ret_ref[j].astype(BF16)
    att = [att_ref[j] for j in range(N_ATT)]
    inv = lax.rsqrt(sum(jnp.sum(t * t, axis=-1, keepdims=True) for t in att) / (N_ATT * LANES) + EPS)
    for j in range(N_ATT):
        a_scr[:, col(N_RET + j)] = (att[j] * inv * attw_ref[j]).astype(BF16)
    for j in range(N_GMLP):
        a_scr[:, col(N_RET + N_ATT + j)] = gm_ref[j].astype(BF16)
    D = x_ref.shape[1]
    ssq = 0.0
    for c in range(D // nc):
        cs = slice(c * nc, (c + 1) * nc)
        x1 = x_ref[:, cs] + jnp.dot(a_scr[...], wb[:, cs], preferred_element_type=F32)
        x1_ref[:, cs] = x1
        ssq = ssq + jnp.sum(x1 * x1, axis=-1, keepdims=True)
    h2_ref[...] = (x1_ref[...] * lax.rsqrt(ssq / D + EPS) * n2w_ref[...]).astype(BF16)


def _outproj(ret, att, gm, x, att_w, w, l, n2w, tm=512, nc=512):
    S, D = x.shape
    K = w.shape[1]
    rows = lambda n: pl.BlockSpec((n, tm, LANES), lambda i: (0, i, 0))
    full = lambda: pl.BlockSpec((tm, D), lambda i: (i, 0))
    const = lambda shape: pl.BlockSpec(shape, lambda i: (0,) * len(shape))
    return pl.pallas_call(
        functools.partial(_outproj_kernel, nc=nc),
        grid=(S // tm,),
        in_specs=[rows(N_RET), rows(N_ATT), rows(N_GMLP), full(), const((N_ATT, 1, LANES)),
                  pl.BlockSpec((None, K, D), lambda i: (l, 0, 0), pipeline_mode=pl.Buffered(1)), const((1, D))],
        out_specs=[full(), full()],
        out_shape=[jax.ShapeDtypeStruct((S, D), F32), jax.ShapeDtypeStruct((S, D), BF16)],
        scratch_shapes=[pltpu.VMEM((tm, K), BF16)],
        compiler_params=_params("parallel"),
        name="outproj",
    )(ret, att, gm, x, att_w.reshape(N_ATT, 1, LANES), w, n2w.reshape(1, D))


def _upproj_kernel(a_ref, wg_ref, wv_ref, cwg_ref, cwv_ref, cbg_ref, cbv_ref, o_ref, wgb, wvb, hg, hv):
    tm = a_ref.shape[0]
    nslab = hg.shape[0]
    head = SUBLANES

    @pl.when(pl.program_id(1) == 0)
    def _():
        wgb[...] = wg_ref[...].astype(BF16)
        wvb[...] = wv_ref[...].astype(BF16)
        hg[:, 0:head, :] = jnp.zeros((nslab, head, LANES), F32)
        hv[:, 0:head, :] = jnp.zeros((nslab, head, LANES), F32)

    a = a_ref[...]

    def conv(wb, cw_ref, cb_ref, hbuf):
        h = jnp.dot(a, wb[...], preferred_element_type=F32)
        outs = []
        for s in range(nslab):
            cs = slice(s * LANES, (s + 1) * LANES)
            hs = h[:, cs]
            hbuf[s, head:head + tm, :] = hs
            outs.append(cb_ref[:, cs] + cw_ref[0:1, cs] * hbuf[s, head - 2:head - 2 + tm, :]
                        + cw_ref[1:2, cs] * hbuf[s, head - 1:head - 1 + tm, :] + cw_ref[2:3, cs] * hs)
            hbuf[s, 0:head, :] = hbuf[s, tm:tm + head, :]
        return outs

    gate = conv(wgb, cwg_ref, cbg_ref, hg)
    val = conv(wvb, cwv_ref, cbv_ref, hv)
    for s in range(nslab):
        o_ref[:, s * LANES:(s + 1) * LANES] = (_silu(gate[s]) * val[s]).astype(o_ref.dtype)


def _upproj(h2, w_up, conv_w, conv_b, l, tm=1024, tn=512):
    S, D = h2.shape
    F = w_up.shape[2] // 2
    nj = F // tn
    taps = conv_w.shape[1]
    hbuf = pltpu.VMEM((tn // LANES, tm + SUBLANES, LANES), F32)
    return pl.pallas_call(
        _upproj_kernel,
        grid=(nj, S // tm),
        in_specs=[pl.BlockSpec((tm, D), lambda j, i: (i, 0)),
                  pl.BlockSpec((None, D, tn), lambda j, i: (l, 0, j)),
                  pl.BlockSpec((None, D, tn), lambda j, i: (l, 0, j + nj)),
                  pl.BlockSpec((None, taps, tn), lambda j, i: (l, 0, j)),
                  pl.BlockSpec((None, taps, tn), lambda j, i: (l, 0, j + nj)),
                  pl.BlockSpec((None, 1, tn), lambda j, i: (l, 0, j)),
                  pl.BlockSpec((None, 1, tn), lambda j, i: (l, 0, j + nj))],
        out_specs=pl.BlockSpec((tm, tn), lambda j, i: (i, j)),
        out_shape=jax.ShapeDtypeStruct((S, F), BF16),
        scratch_shapes=[pltpu.VMEM((D, tn), BF16), pltpu.VMEM((D, tn), BF16), hbuf, hbuf],
        compiler_params=_params("parallel", "arbitrary"),
        name="upproj",
    )(h2, w_up, w_up, conv_w, conv_w, conv_b[:, None, :], conv_b[:, None, :])


def _downproj_kernel(a_ref, w_ref, x1_ref, o_ref, wb):
    @pl.when(pl.program_id(1) == 0)
    def _():
        wb[...] = w_ref[...].astype(BF16)

    o_ref[...] = x1_ref[...] + jnp.dot(a_ref[...], wb[...], preferred_element_type=F32)


def _downproj(g, w, l, x1, tm=512, tn=512):
    S, F = g.shape
    D = w.shape[2]
    tile = lambda: pl.BlockSpec((tm, tn), lambda j, i: (i, j))
    return pl.pallas_call(
        _downproj_kernel,
        grid=(D // tn, S // tm),
        in_specs=[pl.BlockSpec((tm, F), lambda j, i: (i, 0)), pl.BlockSpec((None, F, tn), lambda j, i: (l, 0, j)),
                  tile()],
        out_specs=tile(),
        out_shape=jax.ShapeDtypeStruct((S, D), F32),
        scratch_shapes=[pltpu.VMEM((F, tn), BF16)],
        compiler_params=_params("parallel", "arbitrary"),
        name="downproj",
    )(g, w, x1)


def kernel(x, norm1_w, w_in, ret_norm_w, att_norm_w, gmlp_ln_w, gmlp_ws, gmlp_bs, gmlp_out_w, w_out, norm2_w, w_up, conv_w, conv_b, w_down, final_norm_w):
    B, S, D = x.shape
    depth = w_in.shape[0]
    assert B == 1 and S % SPAN == 0 and w_in.shape[2] == N_SLABS * LANES
    tables = _retention_tables(S)
    xs = x.reshape(S, D)
    w_out_b = w_out.astype(BF16)
    for l in range(depth):
        proj = _inproj(xs, norm1_w[l], w_in, l)
        ret = _retention(proj, tables, ret_norm_w[l])
        att = _dilated_attention(proj)
        gm = _gmlp(proj, gmlp_ln_w[l], gmlp_ws[l], gmlp_bs[l], gmlp_out_w[l])
        x1, h2 = _outproj(ret, att, gm, xs, att_norm_w[l], w_out_b, l, norm2_w[l])
        g = _upproj(h2, w_up, conv_w, conv_b, l)
        xs = _downproj(g, w_down, l, x1)
    return _rmsnorm(xs, final_norm_w).reshape(B, S, D)
```

```python
import functools

import jax
import jax.numpy as jnp
from jax import lax
from jax.experimental import pallas as pl
from jax.experimental.pallas import tpu as pltpu

F32 = jnp.float32
BF16 = jnp.bfloat16

LANES = 128
SUBLANES = 8
CHUNK = 128
N_RET = 6
N_ATT = 6
N_GMLP = 4
DILATIONS = (1, 4, 16)
SPAN = CHUNK * DILATIONS[-1]
ROPE_BASE = 10000.0
EPS = 1e-6
NEG_INF = -1e30
VMEM_LIMIT = 56 * 1024 * 1024

RQ, RK, RV, RG = 0, N_RET, 2 * N_RET, 3 * N_RET
AQ = 4 * N_RET
AK = AQ + N_ATT
AV = AK + N_ATT
GU = AV + N_ATT
GV = GU + N_GMLP
N_SLABS = GV + N_GMLP


def _params(*sem):
    return pltpu.CompilerParams(dimension_semantics=sem, vmem_limit_bytes=VMEM_LIMIT)


def _rms(x, w):
    return x * lax.rsqrt(jnp.mean(x * x, axis=-1, keepdims=True) + EPS) * w


def _silu(x):
    hx = 0.5 * x
    return hx + hx * jnp.tanh(hx)


def _rmsnorm_kernel(x_ref, w_ref, o_ref):
    o_ref[...] = _rms(x_ref[...], w_ref[...]).astype(o_ref.dtype)


def _rmsnorm(x, w, tm=512):
    S, D = x.shape
    return pl.pallas_call(
        _rmsnorm_kernel,
        grid=(S // tm,),
        in_specs=[pl.BlockSpec((tm, D), lambda i: (i, 0)), pl.BlockSpec((1, D), lambda i: (0, 0))],
        out_specs=pl.BlockSpec((tm, D), lambda i: (i, 0)),
        out_shape=jax.ShapeDtypeStruct((S, D), x.dtype),
        compiler_params=_params("parallel"),
        name="rmsnorm",
    )(x, w.reshape(1, D))


def _prescale_kernel(x_ref, w_ref, xw_ref, ssq_ref):
    x = x_ref[...]
    xw_ref[...] = (x * w_ref[...]).astype(BF16)
    ssq_ref[...] = jnp.sum(x * x, axis=-1, keepdims=True)


def _prescale(x, w, tm=512):
    S, D = x.shape
    return pl.pallas_call(
        _prescale_kernel,
        grid=(S // tm,),
        in_specs=[pl.BlockSpec((tm, D), lambda i: (i, 0)), pl.BlockSpec((1, D), lambda i: (0, 0))],
        out_specs=[pl.BlockSpec((tm, D), lambda i: (i, 0)), pl.BlockSpec((None, tm, 1), lambda i: (0, i, 0))],
        out_shape=[jax.ShapeDtypeStruct((S, D), BF16), jax.ShapeDtypeStruct((1, S, 1), F32)],
        compiler_params=_params("parallel"),
        name="prescale",
    )(x, w.reshape(1, D))


def _inproj_kernel(a_ref, ssq_ref, w_ref, o_ref, wb):
    @pl.when(pl.program_id(1) == 0)
    def _():
        wb[...] = w_ref[...].astype(BF16)

    D = a_ref.shape[1]
    r = lax.rsqrt(sum(ssq_ref[p] for p in range(ssq_ref.shape[0])) / D + EPS)
    acc = jnp.dot(a_ref[...], wb[...], preferred_element_type=F32)
    for j in range(o_ref.shape[0]):
        o_ref[j] = acc[:, j * LANES:(j + 1) * LANES] * r


def _inproj(xw, ssq, w, l, tm=1024, tn=1280):
    S, D = xw.shape
    N = w.shape[2]
    parts = ssq.shape[0]
    return pl.pallas_call(
        _inproj_kernel,
        grid=(N // tn, S // tm),
        in_specs=[pl.BlockSpec((tm, D), lambda j, i: (i, 0)), pl.BlockSpec((parts, tm, 1), lambda j, i: (0, i, 0)),
                  pl.BlockSpec((None, D, tn), lambda j, i: (l, 0, j))],
        out_specs=pl.BlockSpec((tn // LANES, tm, LANES), lambda j, i: (j, i, 0)),
        out_shape=jax.ShapeDtypeStruct((N // LANES, S, LANES), F32),
        scratch_shapes=[pltpu.VMEM((D, tn), BF16)],
        compiler_params=_params("parallel", "arbitrary"),
        name="inproj",
    )(xw, ssq, w)


def _retention_kernel(q_ref, k_ref, v_ref, g_ref, cos_ref, sin_ref, dec_ref, xi_ref, zeta_ref, cd_ref,
                      nw_ref, o_ref, state_ref, lbuf, ubuf, sbuf):
    @pl.when(pl.program_id(1) == 0)
    def _():
        state_ref[...] = jnp.zeros_like(state_ref)

    scale = LANES ** -0.5
    nh, tr = q_ref.shape[0], q_ref.shape[1]
    nc = tr // CHUNK
    for c in range(nc):
        rows = pl.ds(c * CHUNK, CHUNK)
        cos, sin = cos_ref[rows, :], sin_ref[rows, :]
        for h in range(nh):
            u = c * nh + h
            q, k = q_ref[h, rows, :], k_ref[h, rows, :]
            qr = q * cos + pltpu.roll(q, LANES // 2, 1) * sin
            kr = (k * cos + pltpu.roll(k, LANES // 2, 1) * sin) * scale
            vb = v_ref[h, rows, :].astype(BF16)
            s = lax.dot_general(qr.astype(BF16), kr.astype(BF16), (((1,), (1,)), ((), ())),
                                preferred_element_type=F32) * dec_ref[h]
            lbuf[u, :, 0:CHUNK] = s.astype(BF16)
            lbuf[u, :, CHUNK:CHUNK + LANES] = (qr * xi_ref[h]).astype(BF16)
            ubuf[u] = lax.dot_general((kr * zeta_ref[h]).astype(BF16), vb, (((0,), (0,)), ((), ())),
                                      preferred_element_type=F32)
    for h in range(nh):
        state = state_ref[h]
        for c in range(nc):
            sbuf[c * nh + h] = state.astype(BF16)
            state = cd_ref[h] * state + ubuf[c * nh + h]
        state_ref[h] = state
    for c in range(nc):
        rows = pl.ds(c * CHUNK, CHUNK)
        for h in range(nh):
            u = c * nh + h
            rhs = jnp.concatenate([v_ref[h, rows, :].astype(BF16), sbuf[u]], axis=0)
            o = jnp.dot(lbuf[u], rhs, preferred_element_type=F32)
            o_ref[h, rows, :] = _silu(g_ref[h, rows, :]) * _rms(o, nw_ref[h])


def _retention_tables(S):
    half = LANES // 2
    inv_freq = ROPE_BASE ** (-jnp.arange(half, dtype=F32) / half)
    ang = jnp.arange(S, dtype=jnp.int32).astype(F32)[:, None] * inv_freq[None, :]
    cos, sin = jnp.cos(ang), jnp.sin(ang)
    cosf = jnp.concatenate([cos, cos], axis=-1)
    sinf = jnp.concatenate([-sin, sin], axis=-1)
    log_g = jnp.log1p(-jnp.exp2(-5.0 - jnp.arange(N_RET, dtype=F32)))
    idx = jnp.arange(CHUNK, dtype=F32)
    diff = idx[:, None] - idx[None, :]
    dec = jnp.where(diff[None] >= 0, jnp.exp(jnp.maximum(diff, 0.0)[None] * log_g[:, None, None]), 0.0)
    xi = jnp.exp((idx[None, :] + 1.0) * log_g[:, None])
    zeta = jnp.exp((CHUNK - 1.0 - idx)[None, :] * log_g[:, None])
    cd = jnp.exp(CHUNK * log_g)
    bcast = lambda t: jnp.broadcast_to(t[:, :, None], (N_RET, CHUNK, LANES))
    return cosf, sinf, dec, bcast(xi), bcast(zeta), jnp.broadcast_to(cd[:, None, None], (N_RET, 1, LANES))


def _retention(proj, tables, norm_w, tr=512, nh=N_RET):
    S = proj.shape[1]
    cosf, sinf, dec, xi, zeta, cd = tables
    assert all(off % nh == 0 for off in (RQ, RK, RV, RG)) and N_RET % nh == 0
    slab = lambda off: pl.BlockSpec((nh, tr, LANES), lambda h, i: (off // nh + h, i, 0))
    tab = pl.BlockSpec((tr, LANES), lambda h, i: (i, 0))
    per_head = lambda r: pl.BlockSpec((nh, r, LANES), lambda h, i: (h, 0, 0))
    return pl.pallas_call(
        _retention_kernel,
        grid=(N_RET // nh, S // tr),
        in_specs=[slab(RQ), slab(RK), slab(RV), slab(RG), tab, tab,
                  per_head(CHUNK), per_head(CHUNK), per_head(CHUNK), per_head(1), per_head(1)],
        out_specs=pl.BlockSpec((nh, tr, LANES), lambda h, i: (h, i, 0)),
        out_shape=jax.ShapeDtypeStruct((N_RET, S, LANES), F32),
        scratch_shapes=[pltpu.VMEM((nh, LANES, LANES), F32),
                        pltpu.VMEM((nh * tr // CHUNK, CHUNK, CHUNK + LANES), BF16),
                        pltpu.VMEM((nh * tr // CHUNK, LANES, LANES), F32),
                        pltpu.VMEM((nh * tr // CHUNK, LANES, LANES), BF16)],
        compiler_params=_params("parallel", "arbitrary"),
        name="retention",
    )(proj, proj, proj, proj, cosf, sinf, dec, xi, zeta, cd, norm_w.reshape(N_RET, 1, LANES))


D2 = DILATIONS[1]
TROWS = SPAN // D2
UNITS = 3 * (SPAN // CHUNK)
assert DILATIONS == (1, D2, D2 * D2) and SPAN == CHUNK * D2 * D2


def _dilated_kernel(q_ref, k_ref, v_ref, o_ref, tq, tk, tv, kt, vt, sbuf, o1, d1, m1, o2, d2, m2, o3, d3, m3):
    n = pl.program_id(1)
    slot = n % 2
    pslot = 1 - slot
    nblk = SPAN // CHUNK
    scale = LANES ** -0.5
    blk = lambda i: pl.ds(i * CHUNK, CHUNK)
    strided = lambda start: pl.ds(start, CHUNK, stride=D2)

    @pl.when(n == 0)
    def _():
        tk[1] = jnp.zeros(tk.shape[1:], F32)
        tv[1] = jnp.zeros(tv.shape[1:], F32)
        kt[...] = jnp.zeros(kt.shape, F32)
        vt[...] = jnp.zeros(vt.shape, F32)

    for r in range(D2):
        tq[r] = q_ref[pl.ds(r, TROWS, stride=D2), :]
        tk[slot, r] = k_ref[pl.ds(r, TROWS, stride=D2), :]
        tv[slot, r] = v_ref[pl.ds(r, TROWS, stride=D2), :]

    qi = lax.broadcasted_iota(jnp.int32, (CHUNK, CHUNK), 0)
    ci = lax.broadcasted_iota(jnp.int32, (CHUNK, CHUNK), 1)
    own_ok = ci <= qi
    prev_ok = ci >= qi
    prev_ok_span = ci >= qi + jnp.where(n > 0, 0, CHUNK)

    def units(it):
        ss, r = it // D2, it % D2
        narrow_prev = (lambda kv: kv[2][...]) if it == 0 else (lambda kv: kv[0][blk(it - 1), :])
        mid_prev = ((lambda kv: kv[1][slot, r, blk(ss - 1), :]) if ss > 0 else
                    (lambda kv: kv[1][pslot, r, blk(D2 - 1), :]))
        return (
            (lambda: q_ref[blk(it), :], narrow_prev, lambda kv: kv[0][blk(it), :],
             prev_ok if it > 0 else prev_ok_span, (blk(it), slice(None))),
            (lambda: tq[r, blk(ss), :], mid_prev, lambda kv: kv[1][slot, r, blk(ss), :],
             prev_ok if ss > 0 else prev_ok_span, (r, blk(ss), slice(None))),
            (lambda: tq[r, strided(ss), :], lambda kv: kv[1][pslot, r, strided(ss), :],
             lambda kv: kv[1][slot, r, strided(ss), :], prev_ok_span, (r, strided(ss), slice(None))),
        )

    keys = (k_ref, tk, kt)
    vals = (v_ref, tv, vt)

    for it in range(nblk):
        for p, (load_q, load_prev, load_own, mask_prev, rows), m_s in zip(range(3), units(it), (m1, m2, m3)):
            k = jnp.concatenate([load_prev(keys), load_own(keys)], axis=0).astype(BF16)
            s = lax.dot_general(load_q().astype(BF16), k, (((1,), (1,)), ((), ())),
                                preferred_element_type=F32)
            sp = jnp.where(mask_prev, s[:, :CHUNK], NEG_INF)
            so = jnp.where(own_ok, s[:, CHUNK:], NEG_INF)
            sbuf[3 * it + p, :, 0:CHUNK] = sp
            sbuf[3 * it + p, :, CHUNK:2 * CHUNK] = so
            m_s[rows] = jnp.broadcast_to(jnp.max(jnp.maximum(sp, so), axis=-1, keepdims=True), (CHUNK, LANES))

    ones = jnp.ones((2 * CHUNK, LANES), BF16)
    for it in range(nblk):
        for p, (_, load_prev, load_own, _, rows), o_s, d_s, m_s in zip(
                range(3), units(it), (o1, o2, o3), (d1, d2, d3), (m1, m2, m3)):
            m = m_s[rows]
            e = jnp.concatenate([jnp.exp((sbuf[3 * it + p, :, 0:CHUNK] - m) * scale),
                                 jnp.exp((sbuf[3 * it + p, :, CHUNK:2 * CHUNK] - m) * scale)], axis=1)
            v = jnp.concatenate([load_prev(vals), load_own(vals)], axis=0).astype(BF16)
            oa = jnp.dot(e.astype(BF16), jnp.concatenate([v, ones], axis=1), preferred_element_type=F32)
            o_s[rows] = oa[:, :LANES]
            d_s[rows] = oa[:, LANES:]

    kt[...] = k_ref[blk(nblk - 1), :]
    vt[...] = v_ref[blk(nblk - 1), :]

    def phase_c(j, carry):
        for r in range(D2):
            nat = (pl.ds(j * CHUNK * D2 + r, CHUNK, stride=D2), slice(None))
            trw = (r, pl.ds(j * CHUNK, CHUNK), slice(None))
            ma, mb, mc = m1[nat], m2[trw], m3[trw]
            mx = jnp.maximum(jnp.maximum(ma, mb), mc)
            wa, wb, wc = (jnp.exp((t - mx) * scale) for t in (ma, mb, mc))
            num = wa * o1[nat] + wb * o2[trw] + wc * o3[trw]
            den = wa * d1[nat] + wb * d2[trw] + wc * d3[trw]
            o_ref[nat] = num / den
        return carry

    lax.fori_loop(0, TROWS // CHUNK, phase_c, 0)


def _dilated_attention(proj):
    S = proj.shape[1]
    slab = lambda off: pl.BlockSpec((None, SPAN, LANES), lambda h, n: (off + h, n, 0))
    nat = pltpu.VMEM((SPAN, LANES), F32)
    res = pltpu.VMEM((D2, TROWS, LANES), F32)
    ring = pltpu.VMEM((2, D2, TROWS, LANES), F32)
    tail = pltpu.VMEM((CHUNK, LANES), F32)
    return pl.pallas_call(
        _dilated_kernel,
        grid=(N_ATT, S // SPAN),
        in_specs=[slab(AQ), slab(AK), slab(AV)],
        out_specs=pl.BlockSpec((None, SPAN, LANES), lambda h, n: (h, n, 0)),
        out_shape=jax.ShapeDtypeStruct((N_ATT, S, LANES), F32),
        scratch_shapes=[res, ring, ring, tail, tail, pltpu.VMEM((UNITS, CHUNK, 2 * CHUNK), F32),
                        nat, nat, nat, res, res, res, res, res, res],
        compiler_params=_params("parallel", "arbitrary"),
        name="dilated_attention",
    )(proj, proj, proj)


def _gmlp_kernel(*refs):
    G = N_GMLP
    u_refs, v_refs = refs[:G], refs[G:2 * G]
    lnw_ref, ws_ref, bs_ref, ow_ref, o_ref = refs[2 * G:]
    tm = o_ref.shape[1]
    width = G * LANES
    gv = [jax.nn.gelu(r[...]) for r in v_refs]
    mu = sum(jnp.sum(t, axis=-1, keepdims=True) for t in gv) / width
    xc = [t - mu for t in gv]
    inv = lax.rsqrt(sum(jnp.sum(t * t, axis=-1, keepdims=True) for t in xc) / width + EPS)
    qi = lax.broadcasted_iota(jnp.int32, (CHUNK, CHUNK), 0)
    ki = lax.broadcasted_iota(jnp.int32, (CHUNK, CHUNK), 1)
    outs = []
    for g in range(G):
        vn = (xc[g] * inv * lnw_ref[g]).astype(BF16)
        wm = jnp.where(ki <= qi, ws_ref[g], 0.0).astype(BF16)
        sp = [jnp.dot(wm, vn[c * CHUNK:(c + 1) * CHUNK, :], preferred_element_type=F32) + bs_ref[g]
              for c in range(tm // CHUNK)]
        outs.append(jax.nn.gelu(u_refs[g][...]) * jnp.concatenate(sp, axis=0))
    inv_o = lax.rsqrt(sum(jnp.sum(t * t, axis=-1, keepdims=True) for t in outs) / width + EPS)
    for g in range(G):
        o_ref[g] = outs[g] * inv_o * ow_ref[g]


def _gmlp(proj, ln_w, ws, bs, out_w, tm=512):
    S = proj.shape[1]
    G = N_GMLP
    slab = lambda s: pl.BlockSpec((None, tm, LANES), lambda i: (s, i, 0))
    small = lambda shape: pl.BlockSpec(shape, lambda i: (0,) * len(shape))
    return pl.pallas_call(
        _gmlp_kernel,
        grid=(S // tm,),
        in_specs=[slab(GU + g) for g in range(G)] + [slab(GV + g) for g in range(G)]
        + [small((G, 1, LANES)), small((G, CHUNK, CHUNK)), small((G, CHUNK, 1)), small((G, 1, LANES))],
        out_specs=pl.BlockSpec((G, tm, LANES), lambda i: (0, i, 0)),
        out_shape=jax.ShapeDtypeStruct((G, S, LANES), F32),
        compiler_params=_params("parallel"),
        name="gmlp",
    )(*([proj] * (2 * G)), ln_w.reshape(G, 1, LANES), ws, bs.reshape(G, CHUNK, 1), out_w.reshape(G, 1, LANES))


def _outproj_kernel(ret_ref, att_ref, gm_ref, x_ref, attw_ref, wb, n2w_ref, x1_ref, h2_ref, a_scr, *, nc):
    col = lambda j: slice(j * LANES, (j + 1) * LANES)
    for j in range(N_RET):
        a_scr[:, col(j)] = ret_ref[j].astype(BF16)
    att = [att_ref[j] for j in range(N_ATT)]
    inv = lax.rsqrt(sum(jnp.sum(t * t, axis=-1, keepdims=True) for t in att) / (N_ATT * LANES) + EPS)
    for j in range(N_ATT):
        a_scr[:, col(N_RET + j)] = (att[j] * inv * attw_ref[j]).astype(BF16)
    for j in range(N_GMLP):
        a_scr[:, col(N_RET + N_ATT + j)] = gm_ref[j].astype(BF16)
    D = x_ref.shape[1]
    ssq = 0.0
    for c in range(D // nc):
        cs = slice(c * nc, (c + 1) * nc)
        x1 = x_ref[:, cs] + jnp.dot(a_scr[...], wb[:, cs], preferred_element_type=F32)
        x1_ref[:, cs] = x1
        ssq = ssq + jnp.sum(x1 * x1, axis=-1, keepdims=True)
    h2_ref[...] = (x1_ref[...] * lax.rsqrt(ssq / D + EPS) * n2w_ref[...]).astype(BF16)


def _outproj(ret, att, gm, x, att_w, w, l, n2w, tm=512, nc=512):
    S, D = x.shape
    K = w.shape[1]
    rows = lambda n: pl.BlockSpec((n, tm, LANES), lambda i: (0, i, 0))
    full = lambda: pl.BlockSpec((tm, D), lambda i: (i, 0))
    const = lambda shape: pl.BlockSpec(shape, lambda i: (0,) * len(shape))
    return pl.pallas_call(
        functools.partial(_outproj_kernel, nc=nc),
        grid=(S // tm,),
        in_specs=[rows(N_RET), rows(N_ATT), rows(N_GMLP), full(), const((N_ATT, 1, LANES)),
                  pl.BlockSpec((None, K, D), lambda i: (l, 0, 0), pipeline_mode=pl.Buffered(1)), const((1, D))],
        out_specs=[full(), full()],
        out_shape=[jax.ShapeDtypeStruct((S, D), F32), jax.ShapeDtypeStruct((S, D), BF16)],
        scratch_shapes=[pltpu.VMEM((tm, K), BF16)],
        compiler_params=_params("parallel"),
        name="outproj",
    )(ret, att, gm, x, att_w.reshape(N_ATT, 1, LANES), w, n2w.reshape(1, D))


def _upproj_kernel(a_ref, wg_ref, wv_ref, cwg_ref, cwv_ref, cbg_ref, cbv_ref, o_ref, wgb, wvb, hg, hv):
    tm = a_ref.shape[0]
    nslab = hg.shape[0]
    head = SUBLANES

    @pl.when(pl.program_id(1) == 0)
    def _():
        wgb[...] = wg_ref[...].astype(BF16)
        wvb[...] = wv_ref[...].astype(BF16)
        hg[:, 0:head, :] = jnp.zeros((nslab, head, LANES), F32)
        hv[:, 0:head, :] = jnp.zeros((nslab, head, LANES), F32)

    a = a_ref[...]

    def conv(wb, cw_ref, cb_ref, hbuf):
        h = jnp.dot(a, wb[...], preferred_element_type=F32)
        outs = []
        for s in range(nslab):
            cs = slice(s * LANES, (s + 1) * LANES)
            hs = h[:, cs]
            hbuf[s, head:head + tm, :] = hs
            outs.append(cb_ref[:, cs] + cw_ref[0:1, cs] * hbuf[s, head - 2:head - 2 + tm, :]
                        + cw_ref[1:2, cs] * hbuf[s, head - 1:head - 1 + tm, :] + cw_ref[2:3, cs] * hs)
            hbuf[s, 0:head, :] = hbuf[s, tm:tm + head, :]
        return outs

    gate = conv(wgb, cwg_ref, cbg_ref, hg)
    val = conv(wvb, cwv_ref, cbv_ref, hv)
    for s in range(nslab):
        o_ref[:, s * LANES:(s + 1) * LANES] = (_silu(gate[s]) * val[s]).astype(o_ref.dtype)


def _upproj(h2, w_up, conv_w, conv_b, l, tm=1024, tn=512):
    S, D = h2.shape
    F = w_up.shape[2] // 2
    nj = F // tn
    taps = conv_w.shape[1]
    hbuf = pltpu.VMEM((tn // LANES, tm + SUBLANES, LANES), F32)
    return pl.pallas_call(
        _upproj_kernel,
        grid=(nj, S // tm),
        in_specs=[pl.BlockSpec((tm, D), lambda j, i: (i, 0)),
                  pl.BlockSpec((None, D, tn), lambda j, i: (l, 0, j)),
                  pl.BlockSpec((None, D, tn), lambda j, i: (l, 0, j + nj)),
                  pl.BlockSpec((None, taps, tn), lambda j, i: (l, 0, j)),
                  pl.BlockSpec((None, taps, tn), lambda j, i: (l, 0, j + nj)),
                  pl.BlockSpec((None, 1, tn), lambda j, i: (l, 0, j)),
                  pl.BlockSpec((None, 1, tn), lambda j, i: (l, 0, j + nj))],
        out_specs=pl.BlockSpec((tm, tn), lambda j, i: (i, j)),
        out_shape=jax.ShapeDtypeStruct((S, F), BF16),
        scratch_shapes=[pltpu.VMEM((D, tn), BF16), pltpu.VMEM((D, tn), BF16), hbuf, hbuf],
        compiler_params=_params("parallel", "arbitrary"),
        name="upproj",
    )(h2, w_up, w_up, conv_w, conv_w, conv_b[:, None, :], conv_b[:, None, :])


def _downproj_kernel(a_ref, w_ref, x1_ref, nw_ref, o_ref, xw_ref, ssq_ref, wb):
    @pl.when(pl.program_id(1) == 0)
    def _():
        wb[...] = w_ref[...].astype(BF16)

    x2 = x1_ref[...] + jnp.dot(a_ref[...], wb[...], preferred_element_type=F32)
    o_ref[...] = x2
    xw_ref[...] = (x2 * nw_ref[...]).astype(BF16)
    ssq_ref[...] = jnp.sum(x2 * x2, axis=-1, keepdims=True)


def _downproj(g, w, l, x1, next_w, tm=512, tn=512):
    S, F = g.shape
    D = w.shape[2]
    tile = lambda: pl.BlockSpec((tm, tn), lambda j, i: (i, j))
    return pl.pallas_call(
        _downproj_kernel,
        grid=(D // tn, S // tm),
        in_specs=[pl.BlockSpec((tm, F), lambda j, i: (i, 0)), pl.BlockSpec((None, F, tn), lambda j, i: (l, 0, j)),
                  tile(), pl.BlockSpec((1, tn), lambda j, i: (0, j))],
        out_specs=[tile(), tile(), pl.BlockSpec((None, tm, 1), lambda j, i: (j, i, 0))],
        out_shape=[jax.ShapeDtypeStruct((S, D), F32), jax.ShapeDtypeStruct((S, D), BF16),
                   jax.ShapeDtypeStruct((D // tn, S, 1), F32)],
        scratch_shapes=[pltpu.VMEM((F, tn), BF16)],
        compiler_params=_params("parallel", "arbitrary"),
        name="downproj",
    )(g, w, x1, next_w.reshape(1, D))


def kernel(x, norm1_w, w_in, ret_norm_w, att_norm_w, gmlp_ln_w, gmlp_ws, gmlp_bs, gmlp_out_w, w_out, norm2_w, w_up, conv_w, conv_b, w_down, final_norm_w):
    B, S, D = x.shape
    depth = w_in.shape[0]
    assert B == 1 and S % SPAN == 0 and w_in.shape[2] == N_SLABS * LANES
    tables = _retention_tables(S)
    xs = x.reshape(S, D)
    w_out_b = w_out.astype(BF16)
    next_norm_w = jnp.concatenate([norm1_w[1:], final_norm_w[None]], axis=0)
    xw, ssq = _prescale(xs, norm1_w[0])
    for l in range(depth):
        proj = _inproj(xw, ssq, w_in, l)
        ret = _retention(proj, tables, ret_norm_w[l])
        att = _dilated_attention(proj)
        gm = _gmlp(proj, gmlp_ln_w[l], gmlp_ws[l], gmlp_bs[l], gmlp_out_w[l])
        x1, h2 = _outproj(ret, att, gm, xs, att_norm_w[l], w_out_b, l, norm2_w[l])
        g = _upproj(h2, w_up, conv_w, conv_b, l)
        xs, xw, ssq = _downproj(g, w_down, l, x1, next_norm_w[l])
    return _rmsnorm(xs, final_norm_w).reshape(B, S, D)
```

```python
import functools

import jax
import jax.numpy as jnp
from jax import lax
from jax.experimental import pallas as pl
from jax.experimental.pallas import tpu as pltpu

F32 = jnp.float32
BF16 = jnp.bfloat16

LANES = 128
SUBLANES = 8
CHUNK = 128
N_RET = 6
N_ATT = 6
N_GMLP = 4
DILATIONS = (1, 4, 16)
SPAN = CHUNK * DILATIONS[-1]
ROPE_BASE = 10000.0
EPS = 1e-6
NEG_INF = -1e30
VMEM_LIMIT = 56 * 1024 * 1024

RQ, RK, RV, RG = 0, N_RET, 2 * N_RET, 3 * N_RET
AQ = 4 * N_RET
AK = AQ + N_ATT
AV = AK + N_ATT
GU = AV + N_ATT
GV = GU + N_GMLP
N_SLABS = GV + N_GMLP


def _params(*sem):
    return pltpu.CompilerParams(dimension_semantics=sem, vmem_limit_bytes=VMEM_LIMIT)


def _rms(x, w):
    return x * lax.rsqrt(jnp.mean(x * x, axis=-1, keepdims=True) + EPS) * w


def _silu(x):
    hx = 0.5 * x
    return hx + hx * jnp.tanh(hx)


def _rmsnorm_kernel(x_ref, w_ref, o_ref):
    o_ref[...] = _rms(x_ref[...], w_ref[...]).astype(o_ref.dtype)


def _rmsnorm(x, w, tm=512):
    S, D = x.shape
    return pl.pallas_call(
        _rmsnorm_kernel,
        grid=(S // tm,),
        in_specs=[pl.BlockSpec((tm, D), lambda i: (i, 0)), pl.BlockSpec((1, D), lambda i: (0, 0))],
        out_specs=pl.BlockSpec((tm, D), lambda i: (i, 0)),
        out_shape=jax.ShapeDtypeStruct((S, D), x.dtype),
        compiler_params=_params("parallel"),
        name="rmsnorm",
    )(x, w.reshape(1, D))


def _prescale_kernel(x_ref, w_ref, xw_ref, ssq_ref):
    x = x_ref[...]
    xw_ref[...] = (x * w_ref[...]).astype(BF16)
    ssq_ref[...] = jnp.sum(x * x, axis=-1, keepdims=True)


def _prescale(x, w, tm=512):
    S, D = x.shape
    return pl.pallas_call(
        _prescale_kernel,
        grid=(S // tm,),
        in_specs=[pl.BlockSpec((tm, D), lambda i: (i, 0)), pl.BlockSpec((1, D), lambda i: (0, 0))],
        out_specs=[pl.BlockSpec((tm, D), lambda i: (i, 0)), pl.BlockSpec((None, tm, 1), lambda i: (0, i, 0))],
        out_shape=[jax.ShapeDtypeStruct((S, D), BF16), jax.ShapeDtypeStruct((1, S, 1), F32)],
        compiler_params=_params("parallel"),
        name="prescale",
    )(x, w.reshape(1, D))


def _inproj_kernel(a_ref, ssq_ref, w_ref, o_ref, wb):
    @pl.when(pl.program_id(1) == 0)
    def _():
        wb[...] = w_ref[...].astype(BF16)

    D = a_ref.shape[1]
    r = lax.rsqrt(sum(ssq_ref[p] for p in range(ssq_ref.shape[0])) / D + EPS)
    acc = jnp.dot(a_ref[...], wb[...], preferred_element_type=F32)
    for j in range(o_ref.shape[0]):
        o_ref[j] = acc[:, j * LANES:(j + 1) * LANES] * r


def _inproj(xw, ssq, w, l, tm=1024, tn=1280):
    S, D = xw.shape
    N = w.shape[2]
    parts = ssq.shape[0]
    return pl.pallas_call(
        _inproj_kernel,
        grid=(N // tn, S // tm),
        in_specs=[pl.BlockSpec((tm, D), lambda j, i: (i, 0)), pl.BlockSpec((parts, tm, 1), lambda j, i: (0, i, 0)),
                  pl.BlockSpec((None, D, tn), lambda j, i: (l, 0, j))],
        out_specs=pl.BlockSpec((tn // LANES, tm, LANES), lambda j, i: (j, i, 0)),
        out_shape=jax.ShapeDtypeStruct((N // LANES, S, LANES), F32),
        scratch_shapes=[pltpu.VMEM((D, tn), BF16)],
        compiler_params=_params("parallel", "arbitrary"),
        name="inproj",
    )(xw, ssq, w)


def _retention_kernel(q_ref, k_ref, v_ref, g_ref, cos_ref, sin_ref, dec_ref, xi_ref, zeta_ref, cd_ref,
                      nw_ref, o_ref, state_ref, lbuf, ubuf, sbuf):
    @pl.when(pl.program_id(1) == 0)
    def _():
        state_ref[...] = jnp.zeros_like(state_ref)

    scale = LANES ** -0.5
    nh, tr = q_ref.shape[0], q_ref.shape[1]
    nc = tr // CHUNK
    for c in range(nc):
        rows = pl.ds(c * CHUNK, CHUNK)
        cos, sin = cos_ref[rows, :], sin_ref[rows, :]
        for h in range(nh):
            u = c * nh + h
            q, k = q_ref[h, rows, :], k_ref[h, rows, :]
            qr = q * cos + pltpu.roll(q, LANES // 2, 1) * sin
            kr = (k * cos + pltpu.roll(k, LANES // 2, 1) * sin) * scale
            vb = v_ref[h, rows, :].astype(BF16)
            s = lax.dot_general(qr.astype(BF16), kr.astype(BF16), (((1,), (1,)), ((), ())),
                                preferred_element_type=F32) * dec_ref[h]
            lbuf[u, :, 0:CHUNK] = s.astype(BF16)
            lbuf[u, :, CHUNK:CHUNK + LANES] = (qr * xi_ref[h]).astype(BF16)
            ubuf[u] = lax.dot_general((kr * zeta_ref[h]).astype(BF16), vb, (((0,), (0,)), ((), ())),
                                      preferred_element_type=F32)
    for h in range(nh):
        state = state_ref[h]
        for c in range(nc):
            sbuf[c * nh + h] = state.astype(BF16)
            state = cd_ref[h] * state + ubuf[c * nh + h]
        state_ref[h] = state
    for c in range(nc):
        rows = pl.ds(c * CHUNK, CHUNK)
        for h in range(nh):
            u = c * nh + h
            rhs = jnp.concatenate([v_ref[h, rows, :].astype(BF16), sbuf[u]], axis=0)
            o = jnp.dot(lbuf[u], rhs, preferred_element_type=F32)
            o_ref[h, rows, :] = _silu(g_ref[h, rows, :]) * _rms(o, nw_ref[h])


def _retention_tables(S):
    half = LANES // 2
    inv_freq = ROPE_BASE ** (-jnp.arange(half, dtype=F32) / half)
    ang = jnp.arange(S, dtype=jnp.int32).astype(F32)[:, None] * inv_freq[None, :]
    cos, sin = jnp.cos(ang), jnp.sin(ang)
    cosf = jnp.concatenate([cos, cos], axis=-1)
    sinf = jnp.concatenate([-sin, sin], axis=-1)
    log_g = jnp.log1p(-jnp.exp2(-5.0 - jnp.arange(N_RET, dtype=F32)))
    idx = jnp.arange(CHUNK, dtype=F32)
    diff = idx[:, None] - idx[None, :]
    dec = jnp.where(diff[None] >= 0, jnp.exp(jnp.maximum(diff, 0.0)[None] * log_g[:, None, None]), 0.0)
    xi = jnp.exp((idx[None, :] + 1.0) * log_g[:, None])
    zeta = jnp.exp((CHUNK - 1.0 - idx)[None, :] * log_g[:, None])
    cd = jnp.exp(CHUNK * log_g)
    bcast = lambda t: jnp.broadcast_to(t[:, :, None], (N_RET, CHUNK, LANES))
    return cosf, sinf, dec, bcast(xi), bcast(zeta), jnp.broadcast_to(cd[:, None, None], (N_RET, 1, LANES))


def _retention(proj, tables, norm_w, tr=1024, nh=N_RET):
    S = proj.shape[1]
    cosf, sinf, dec, xi, zeta, cd = tables
    assert all(off % nh == 0 for off in (RQ, RK, RV, RG)) and N_RET % nh == 0
    slab = lambda off: pl.BlockSpec((nh, tr, LANES), lambda h, i: (off // nh + h, i, 0))
    tab = pl.BlockSpec((tr, LANES), lambda h, i: (i, 0))
    per_head = lambda r: pl.BlockSpec((nh, r, LANES), lambda h, i: (h, 0, 0))
    return pl.pallas_call(
        _retention_kernel,
        grid=(N_RET // nh, S // tr),
        in_specs=[slab(RQ), slab(RK), slab(RV), slab(RG), tab, tab,
                  per_head(CHUNK), per_head(CHUNK), per_head(CHUNK), per_head(1), per_head(1)],
        out_specs=pl.BlockSpec((nh, tr, LANES), lambda h, i: (h, i, 0)),
        out_shape=jax.ShapeDtypeStruct((N_RET, S, LANES), F32),
        scratch_shapes=[pltpu.VMEM((nh, LANES, LANES), F32),
                        pltpu.VMEM((nh * tr // CHUNK, CHUNK, CHUNK + LANES), BF16),
                        pltpu.VMEM((nh * tr // CHUNK, LANES, LANES), F32),
                        pltpu.VMEM((nh * tr // CHUNK, LANES, LANES), BF16)],
        compiler_params=_params("parallel", "arbitrary"),
        name="retention",
    )(proj, proj, proj, proj, cosf, sinf, dec, xi, zeta, cd, norm_w.reshape(N_RET, 1, LANES))


D2 = DILATIONS[1]
TROWS = SPAN // D2
UNITS = 3 * (SPAN // CHUNK)
assert DILATIONS == (1, D2, D2 * D2) and SPAN == CHUNK * D2 * D2


def _dilated_kernel(q_ref, k_ref, v_ref, o_ref, tq, tk, tv, kt, vt, sbuf, o1, d1, m1, o2, d2, m2, o3, d3, m3):
    n = pl.program_id(1)
    slot = n % 2
    pslot = 1 - slot
    nblk = SPAN // CHUNK
    scale = LANES ** -0.5
    blk = lambda i: pl.ds(i * CHUNK, CHUNK)
    strided = lambda start: pl.ds(start, CHUNK, stride=D2)

    @pl.when(n == 0)
    def _():
        tk[1] = jnp.zeros(tk.shape[1:], F32)
        tv[1] = jnp.zeros(tv.shape[1:], F32)
        kt[...] = jnp.zeros(kt.shape, F32)
        vt[...] = jnp.zeros(vt.shape, F32)

    for r in range(D2):
        tq[r] = q_ref[pl.ds(r, TROWS, stride=D2), :]
        tk[slot, r] = k_ref[pl.ds(r, TROWS, stride=D2), :]
        tv[slot, r] = v_ref[pl.ds(r, TROWS, stride=D2), :]

    qi = lax.broadcasted_iota(jnp.int32, (CHUNK, CHUNK), 0)
    ci = lax.broadcasted_iota(jnp.int32, (CHUNK, CHUNK), 1)
    own_ok = ci <= qi
    prev_ok = ci >= qi
    prev_ok_span = ci >= qi + jnp.where(n > 0, 0, CHUNK)

    def units(it):
        ss, r = it // D2, it % D2
        narrow_prev = (lambda kv: kv[2][...]) if it == 0 else (lambda kv: kv[0][blk(it - 1), :])
        mid_prev = ((lambda kv: kv[1][slot, r, blk(ss - 1), :]) if ss > 0 else
                    (lambda kv: kv[1][pslot, r, blk(D2 - 1), :]))
        return (
            (lambda: q_ref[blk(it), :], narrow_prev, lambda kv: kv[0][blk(it), :],
             prev_ok if it > 0 else prev_ok_span, (blk(it), slice(None))),
            (lambda: tq[r, blk(ss), :], mid_prev, lambda kv: kv[1][slot, r, blk(ss), :],
             prev_ok if ss > 0 else prev_ok_span, (r, blk(ss), slice(None))),
            (lambda: tq[r, strided(ss), :], lambda kv: kv[1][pslot, r, strided(ss), :],
             lambda kv: kv[1][slot, r, strided(ss), :], prev_ok_span, (r, strided(ss), slice(None))),
        )

    keys = (k_ref, tk, kt)
    vals = (v_ref, tv, vt)

    for it in range(nblk):
        for p, (load_q, load_prev, load_own, mask_prev, rows), m_s in zip(range(3), units(it), (m1, m2, m3)):
            k = jnp.concatenate([load_prev(keys), load_own(keys)], axis=0).astype(BF16)
            s = lax.dot_general(load_q().astype(BF16), k, (((1,), (1,)), ((), ())),
                                preferred_element_type=F32)
            sp = jnp.where(mask_prev, s[:, :CHUNK], NEG_INF)
            so = jnp.where(own_ok, s[:, CHUNK:], NEG_INF)
            sbuf[3 * it + p, :, 0:CHUNK] = sp
            sbuf[3 * it + p, :, CHUNK:2 * CHUNK] = so
            m_s[rows] = jnp.broadcast_to(jnp.max(jnp.maximum(sp, so), axis=-1, keepdims=True), (CHUNK, LANES))

    ones = jnp.ones((2 * CHUNK, LANES), BF16)
    for it in range(nblk):
        for p, (_, load_prev, load_own, _, rows), o_s, d_s, m_s in zip(
                range(3), units(it), (o1, o2, o3), (d1, d2, d3), (m1, m2, m3)):
            m = m_s[rows]
            e = jnp.concatenate([jnp.exp((sbuf[3 * it + p, :, 0:CHUNK] - m) * scale),
                                 jnp.exp((sbuf[3 * it + p, :, CHUNK:2 * CHUNK] - m) * scale)], axis=1)
            v = jnp.concatenate([load_prev(vals), load_own(vals)], axis=0).astype(BF16)
            oa = jnp.dot(e.astype(BF16), jnp.concatenate([v, ones], axis=1), preferred_element_type=F32)
            o_s[rows] = oa[:, :LANES]
            d_s[rows] = oa[:, LANES:]

    kt[...] = k_ref[blk(nblk - 1), :]
    vt[...] = v_ref[blk(nblk - 1), :]

    def phase_c(j, carry):
        for r in range(D2):
            nat = (pl.ds(j * CHUNK * D2 + r, CHUNK, stride=D2), slice(None))
            trw = (r, pl.ds(j * CHUNK, CHUNK), slice(None))
            ma, mb, mc = m1[nat], m2[trw], m3[trw]
            mx = jnp.maximum(jnp.maximum(ma, mb), mc)
            wa, wb, wc = (jnp.exp((t - mx) * scale) for t in (ma, mb, mc))
            num = wa * o1[nat] + wb * o2[trw] + wc * o3[trw]
            den = wa * d1[nat] + wb * d2[trw] + wc * d3[trw]
            o_ref[nat] = num / den
        return carry

    lax.fori_loop(0, TROWS // CHUNK, phase_c, 0)


def _dilated_attention(proj):
    S = proj.shape[1]
    slab = lambda off: pl.BlockSpec((None, SPAN, LANES), lambda h, n: (off + h, n, 0))
    nat = pltpu.VMEM((SPAN, LANES), F32)
    res = pltpu.VMEM((D2, TROWS, LANES), F32)
    ring = pltpu.VMEM((2, D2, TROWS, LANES), F32)
    tail = pltpu.VMEM((CHUNK, LANES), F32)
    return pl.pallas_call(
        _dilated_kernel,
        grid=(N_ATT, S // SPAN),
        in_specs=[slab(AQ), slab(AK), slab(AV)],
        out_specs=pl.BlockSpec((None, SPAN, LANES), lambda h, n: (h, n, 0)),
        out_shape=jax.ShapeDtypeStruct((N_ATT, S, LANES), F32),
        scratch_shapes=[res, ring, ring, tail, tail, pltpu.VMEM((UNITS, CHUNK, 2 * CHUNK), F32),
                        nat, nat, nat, res, res, res, res, res, res],
        compiler_params=_params("parallel", "arbitrary"),
        name="dilated_attention",
    )(proj, proj, proj)


def _gmlp_kernel(*refs):
    G = N_GMLP
    u_refs, v_refs = refs[:G], refs[G:2 * G]
    lnw_ref, ws_ref, bs_ref, ow_ref, o_ref = refs[2 * G:]
    tm = o_ref.shape[1]
    width = G * LANES
    gv = [jax.nn.gelu(r[...]) for r in v_refs]
    mu = sum(jnp.sum(t, axis=-1, keepdims=True) for t in gv) / width
    xc = [t - mu for t in gv]
    inv = lax.rsqrt(sum(jnp.sum(t * t, axis=-1, keepdims=True) for t in xc) / width + EPS)
    qi = lax.broadcasted_iota(jnp.int32, (CHUNK, CHUNK), 0)
    ki = lax.broadcasted_iota(jnp.int32, (CHUNK, CHUNK), 1)
    outs = []
    for g in range(G):
        vn = (xc[g] * inv * lnw_ref[g]).astype(BF16)
        wm = jnp.where(ki <= qi, ws_ref[g], 0.0).astype(BF16)
        sp = [jnp.dot(wm, vn[c * CHUNK:(c + 1) * CHUNK, :], preferred_element_type=F32) + bs_ref[g]
              for c in range(tm // CHUNK)]
        outs.append(jax.nn.gelu(u_refs[g][...]) * jnp.concatenate(sp, axis=0))
    inv_o = lax.rsqrt(sum(jnp.sum(t * t, axis=-1, keepdims=True) for t in outs) / width + EPS)
    for g in range(G):
        o_ref[g] = outs[g] * inv_o * ow_ref[g]


def _gmlp(proj, ln_w, ws, bs, out_w, tm=512):
    S = proj.shape[1]
    G = N_GMLP
    slab = lambda s: pl.BlockSpec((None, tm, LANES), lambda i: (s, i, 0))
    small = lambda shape: pl.BlockSpec(shape, lambda i: (0,) * len(shape))
    return pl.pallas_call(
        _gmlp_kernel,
        grid=(S // tm,),
        in_specs=[slab(GU + g) for g in range(G)] + [slab(GV + g) for g in range(G)]
        + [small((G, 1, LANES)), small((G, CHUNK, CHUNK)), small((G, CHUNK, 1)), small((G, 1, LANES))],
        out_specs=pl.BlockSpec((G, tm, LANES), lambda i: (0, i, 0)),
        out_shape=jax.ShapeDtypeStruct((G, S, LANES), F32),
        compiler_params=_params("parallel"),
        name="gmlp",
    )(*([proj] * (2 * G)), ln_w.reshape(G, 1, LANES), ws, bs.reshape(G, CHUNK, 1), out_w.reshape(G, 1, LANES))


def _outproj_kernel(ret_ref, att_ref, gm_ref, x_ref, attw_ref, wb, n2w_ref, x1_ref, h2_ref, a_scr, *, nc):
    col = lambda j: slice(j * LANES, (j + 1) * LANES)
    for j in range(N_RET):
        a_scr[:, col(j)] = ret_ref[j].astype(BF16)
    att = [att_ref[j] for j in range(N_ATT)]
    inv = lax.rsqrt(sum(jnp.sum(t * t, axis=-1, keepdims=True) for t in att) / (N_ATT * LANES) + EPS)
    for j in range(N_ATT):
        a_scr[:, col(N_RET + j)] = (att[j] * inv * attw_ref[j]).astype(BF16)
    for j in range(N_GMLP):
        a_scr[:, col(N_RET + N_ATT + j)] = gm_ref[j].astype(BF16)
    D = x_ref.shape[1]
    ssq = 0.0
    for c in range(D // nc):
        cs = slice(c * nc, (c + 1) * nc)
        x1 = x_ref[:, cs] + jnp.dot(a_scr[...], wb[:, cs], preferred_element_type=F32)
        x1_ref[:, cs] = x1
        ssq = ssq + jnp.sum(x1 * x1, axis=-1, keepdims=True)
    h2_ref[...] = (x1_ref[...] * lax.rsqrt(ssq / D + EPS) * n2w_ref[...]).astype(BF16)


def _outproj(ret, att, gm, x, att_w, w, l, n2w, tm=512, nc=512):
    S, D = x.shape
    K = w.shape[1]
    rows = lambda n: pl.BlockSpec((n, tm, LANES), lambda i: (0, i, 0))
    full = lambda: pl.BlockSpec((tm, D), lambda i: (i, 0))
    const = lambda shape: pl.BlockSpec(shape, lambda i: (0,) * len(shape))
    return pl.pallas_call(
        functools.partial(_outproj_kernel, nc=nc),
        grid=(S // tm,),
        in_specs=[rows(N_RET), rows(N_ATT), rows(N_GMLP), full(), const((N_ATT, 1, LANES)),
                  pl.BlockSpec((None, K, D), lambda i: (l, 0, 0), pipeline_mode=pl.Buffered(1)), const((1, D))],
        out_specs=[full(), full()],
        out_shape=[jax.ShapeDtypeStruct((S, D), F32), jax.ShapeDtypeStruct((S, D), BF16)],
        scratch_shapes=[pltpu.VMEM((tm, K), BF16)],
        compiler_params=_params("parallel"),
        name="outproj",
    )(ret, att, gm, x, att_w.reshape(N_ATT, 1, LANES), w, n2w.reshape(1, D))


def _upproj_kernel(a_ref, wg_ref, wv_ref, cwg_ref, cwv_ref, cbg_ref, cbv_ref, o_ref, wgb, wvb, hg, hv):
    tm = a_ref.shape[0]
    nslab = hg.shape[0]
    head = SUBLANES

    @pl.when(pl.program_id(1) == 0)
    def _():
        wgb[...] = wg_ref[...].astype(BF16)
        wvb[...] = wv_ref[...].astype(BF16)
        hg[:, 0:head, :] = jnp.zeros((nslab, head, LANES), F32)
        hv[:, 0:head, :] = jnp.zeros((nslab, head, LANES), F32)

    a = a_ref[...]

    def conv(wb, cw_ref, cb_ref, hbuf):
        h = jnp.dot(a, wb[...], preferred_element_type=F32)
        outs = []
        for s in range(nslab):
            cs = slice(s * LANES, (s + 1) * LANES)
            hs = h[:, cs]
            hbuf[s, head:head + tm, :] = hs
            outs.append(cb_ref[:, cs] + cw_ref[0:1, cs] * hbuf[s, head - 2:head - 2 + tm, :]
                        + cw_ref[1:2, cs] * hbuf[s, head - 1:head - 1 + tm, :] + cw_ref[2:3, cs] * hs)
            hbuf[s, 0:head, :] = hbuf[s, tm:tm + head, :]
        return outs

    gate = conv(wgb, cwg_ref, cbg_ref, hg)
    val = conv(wvb, cwv_ref, cbv_ref, hv)
    for s in range(nslab):
        o_ref[:, s * LANES:(s + 1) * LANES] = (_silu(gate[s]) * val[s]).astype(o_ref.dtype)


def _upproj(h2, w_up, conv_w, conv_b, l, tm=1024, tn=512):
    S, D = h2.shape
    F = w_up.shape[2] // 2
    nj = F // tn
    taps = conv_w.shape[1]
    hbuf = pltpu.VMEM((tn // LANES, tm + SUBLANES, LANES), F32)
    return pl.pallas_call(
        _upproj_kernel,
        grid=(nj, S // tm),
        in_specs=[pl.BlockSpec((tm, D), lambda j, i: (i, 0)),
                  pl.BlockSpec((None, D, tn), lambda j, i: (l, 0, j)),
                  pl.BlockSpec((None, D, tn), lambda j, i: (l, 0, j + nj)),
                  pl.BlockSpec((None, taps, tn), lambda j, i: (l, 0, j)),
                  pl.BlockSpec((None, taps, tn), lambda j, i: (l, 0, j + nj)),
                  pl.BlockSpec((None, 1, tn), lambda j, i: (l, 0, j)),
                  pl.BlockSpec((None, 1, tn), lambda j, i: (l, 0, j + nj))],
        out_specs=pl.BlockSpec((tm, tn), lambda j, i: (i, j)),
        out_shape=jax.ShapeDtypeStruct((S, F), BF16),
        scratch_shapes=[pltpu.VMEM((D, tn), BF16), pltpu.VMEM((D, tn), BF16), hbuf, hbuf],
        compiler_params=_params("parallel", "arbitrary"),
        name="upproj",
    )(h2, w_up, w_up, conv_w, conv_w, conv_b[:, None, :], conv_b[:, None, :])


def _downproj_kernel(a_ref, w_ref, x1_ref, *refs):
    x2 = x1_ref[...] + jnp.dot(a_ref[...], w_ref[...], preferred_element_type=F32)
    if len(refs) == 1:
        refs[0][...] = x2
        return
    nw_ref, o_ref, xw_ref, ssq_ref = refs
    o_ref[...] = x2
    xw_ref[...] = (x2 * nw_ref[...]).astype(BF16)
    ssq_ref[...] = jnp.sum(x2 * x2, axis=-1, keepdims=True)


def _downproj(g, w, l, x1, next_w=None, tm=512, tn=1024):
    S, F = g.shape
    D = w.shape[2]
    tile = lambda: pl.BlockSpec((tm, tn), lambda j, i: (i, j))
    in_specs = [pl.BlockSpec((tm, F), lambda j, i: (i, 0)), pl.BlockSpec((None, F, tn), lambda j, i: (l, 0, j)), tile()]
    if next_w is None:
        extra, out_specs, out_shape = (), tile(), jax.ShapeDtypeStruct((S, D), F32)
    else:
        extra = (next_w.reshape(1, D),)
        in_specs.append(pl.BlockSpec((1, tn), lambda j, i: (0, j)))
        out_specs = [tile(), tile(), pl.BlockSpec((None, tm, 1), lambda j, i: (j, i, 0))]
        out_shape = [jax.ShapeDtypeStruct((S, D), F32), jax.ShapeDtypeStruct((S, D), BF16),
                     jax.ShapeDtypeStruct((D // tn, S, 1), F32)]
    return pl.pallas_call(
        _downproj_kernel,
        grid=(D // tn, S // tm),
        in_specs=in_specs,
        out_specs=out_specs,
        out_shape=out_shape,
        compiler_params=_params("parallel", "parallel"),
        name="downproj",
    )(g, w, x1, *extra)


def kernel(x, norm1_w, w_in, ret_norm_w, att_norm_w, gmlp_ln_w, gmlp_ws, gmlp_bs, gmlp_out_w, w_out, norm2_w, w_up, conv_w, conv_b, w_down, final_norm_w):
    B, S, D = x.shape
    depth = w_in.shape[0]
    assert B == 1 and S % SPAN == 0 and w_in.shape[2] == N_SLABS * LANES
    tables = _retention_tables(S)
    xs = x.reshape(S, D)
    w_out_b, w_down_b = w_out.astype(BF16), w_down.astype(BF16)
    xw, ssq = _prescale(xs, norm1_w[0])
    for l in range(depth):
        proj = _inproj(xw, ssq, w_in, l)
        ret = _retention(proj, tables, ret_norm_w[l])
        att = _dilated_attention(proj)
        gm = _gmlp(proj, gmlp_ln_w[l], gmlp_ws[l], gmlp_bs[l], gmlp_out_w[l])
        x1, h2 = _outproj(ret, att, gm, xs, att_norm_w[l], w_out_b, l, norm2_w[l])
        g = _upproj(h2, w_up, conv_w, conv_b, l)
        if l + 1 < depth:
            xs, xw, ssq = _downproj(g, w_down_b, l, x1, norm1_w[l + 1])
        else:
            xs = _downproj(g, w_down_b, l, x1)
    return _rmsnorm(xs, final_norm_w).reshape(B, S, D)
```

```python
import functools

import jax
import jax.numpy as jnp
from jax import lax
from jax.experimental import pallas as pl
from jax.experimental.pallas import tpu as pltpu

F32 = jnp.float32
BF16 = jnp.bfloat16

LANES = 128
SUBLANES = 8
BF16_SUBLANES = 16
CHUNK = 128
N_RET = 6
N_ATT = 6
N_GMLP = 4
DILATIONS = (1, 4, 16)
SPAN = CHUNK * DILATIONS[-1]
ROPE_BASE = 10000.0
EPS = 1e-6
NEG_INF = -1e30
VMEM_LIMIT = 56 * 1024 * 1024

RQ, RK, RV, RG = 0, N_RET, 2 * N_RET, 3 * N_RET
AQ = 4 * N_RET
AK = AQ + N_ATT
AV = AK + N_ATT
GU = AV + N_ATT
GV = GU + N_GMLP
N_SLABS = GV + N_GMLP


def _params(*sem):
    return pltpu.CompilerParams(dimension_semantics=sem, vmem_limit_bytes=VMEM_LIMIT)


def _sidecast(w, l, steps, step_of):
    _, rows, cols = w.shape
    band = rows // steps
    assert band * steps == rows and band % BF16_SUBLANES == 0
    return (pl.BlockSpec((None, band, cols), lambda *g: (l, step_of(*g), 0)),
            pl.BlockSpec((band, cols), lambda *g: (step_of(*g), 0)),
            jax.ShapeDtypeStruct((rows, cols), BF16))


def _rms(x, w):
    return x * lax.rsqrt(jnp.mean(x * x, axis=-1, keepdims=True) + EPS) * w


def _silu(x):
    hx = 0.5 * x
    return hx + hx * jnp.tanh(hx)


def _rmsnorm_kernel(x_ref, w_ref, o_ref):
    o_ref[...] = _rms(x_ref[...], w_ref[...]).astype(o_ref.dtype)


def _rmsnorm(x, w, tm=512):
    S, D = x.shape
    return pl.pallas_call(
        _rmsnorm_kernel,
        grid=(S // tm,),
        in_specs=[pl.BlockSpec((tm, D), lambda i: (i, 0)), pl.BlockSpec((1, D), lambda i: (0, 0))],
        out_specs=pl.BlockSpec((tm, D), lambda i: (i, 0)),
        out_shape=jax.ShapeDtypeStruct((S, D), x.dtype),
        compiler_params=_params("parallel"),
        name="rmsnorm",
    )(x, w.reshape(1, D))


def _prescale_kernel(x_ref, w_ref, wc_ref, xw_ref, ssq_ref, wcb_ref):
    x = x_ref[...]
    xw_ref[...] = (x * w_ref[...]).astype(BF16)
    ssq_ref[...] = jnp.sum(x * x, axis=-1, keepdims=True)
    wcb_ref[...] = wc_ref[...].astype(BF16)


def _prescale(x, w, wc, lc, tm=512):
    S, D = x.shape
    wc_in, wc_out, wc_shape = _sidecast(wc, lc, S // tm, lambda i: i)
    return pl.pallas_call(
        _prescale_kernel,
        grid=(S // tm,),
        in_specs=[pl.BlockSpec((tm, D), lambda i: (i, 0)), pl.BlockSpec((1, D), lambda i: (0, 0)), wc_in],
        out_specs=[pl.BlockSpec((tm, D), lambda i: (i, 0)), pl.BlockSpec((None, tm, 1), lambda i: (0, i, 0)), wc_out],
        out_shape=[jax.ShapeDtypeStruct((S, D), BF16), jax.ShapeDtypeStruct((1, S, 1), F32), wc_shape],
        compiler_params=_params("parallel"),
        name="prescale",
    )(x, w.reshape(1, D), wc)


def _inproj_kernel(a_ref, ssq_ref, w_ref, o_ref, wb):
    @pl.when(pl.program_id(1) == 0)
    def _():
        wb[...] = w_ref[...].astype(BF16)

    D = a_ref.shape[1]
    r = lax.rsqrt(sum(ssq_ref[p] for p in range(ssq_ref.shape[0])) / D + EPS)
    acc = jnp.dot(a_ref[...], wb[...], preferred_element_type=F32)
    for j in range(o_ref.shape[0]):
        o_ref[j] = acc[:, j * LANES:(j + 1) * LANES] * r


def _inproj(xw, ssq, w, l, tm=1024, tn=1280):
    S, D = xw.shape
    N = w.shape[2]
    parts = ssq.shape[0]
    return pl.pallas_call(
        _inproj_kernel,
        grid=(N // tn, S // tm),
        in_specs=[pl.BlockSpec((tm, D), lambda j, i: (i, 0)), pl.BlockSpec((parts, tm, 1), lambda j, i: (0, i, 0)),
                  pl.BlockSpec((None, D, tn), lambda j, i: (l, 0, j))],
        out_specs=pl.BlockSpec((tn // LANES, tm, LANES), lambda j, i: (j, i, 0)),
        out_shape=jax.ShapeDtypeStruct((N // LANES, S, LANES), F32),
        scratch_shapes=[pltpu.VMEM((D, tn), BF16)],
        compiler_params=_params("parallel", "arbitrary"),
        name="inproj",
    )(xw, ssq, w)


def _retention_kernel(q_ref, k_ref, v_ref, g_ref, cos_ref, sin_ref, dec_ref, xi_ref, zeta_ref, cd_ref,
                      nw_ref, o_ref, state_ref, lbuf, ubuf, sbuf):
    @pl.when(pl.program_id(1) == 0)
    def _():
        state_ref[...] = jnp.zeros_like(state_ref)

    scale = LANES ** -0.5
    nh, tr = q_ref.shape[0], q_ref.shape[1]
    nc = tr // CHUNK
    for c in range(nc):
        rows = pl.ds(c * CHUNK, CHUNK)
        cos, sin = cos_ref[rows, :], sin_ref[rows, :]
        for h in range(nh):
            u = c * nh + h
            q, k = q_ref[h, rows, :], k_ref[h, rows, :]
            qr = q * cos + pltpu.roll(q, LANES // 2, 1) * sin
            kr = (k * cos + pltpu.roll(k, LANES // 2, 1) * sin) * scale
            vb = v_ref[h, rows, :].astype(BF16)
            s = lax.dot_general(qr.astype(BF16), kr.astype(BF16), (((1,), (1,)), ((), ())),
                                preferred_element_type=F32) * dec_ref[h]
            lbuf[u, :, 0:CHUNK] = s.astype(BF16)
            lbuf[u, :, CHUNK:CHUNK + LANES] = (qr * xi_ref[h]).astype(BF16)
            ubuf[u] = lax.dot_general((kr * zeta_ref[h]).astype(BF16), vb, (((0,), (0,)), ((), ())),
                                      preferred_element_type=F32)
    for h in range(nh):
        state = state_ref[h]
        for c in range(nc):
            sbuf[c * nh + h] = state.astype(BF16)
            state = cd_ref[h] * state + ubuf[c * nh + h]
        state_ref[h] = state
    for c in range(nc):
        rows = pl.ds(c * CHUNK, CHUNK)
        for h in range(nh):
            u = c * nh + h
            rhs = jnp.concatenate([v_ref[h, rows, :].astype(BF16), sbuf[u]], axis=0)
            o = jnp.dot(lbuf[u], rhs, preferred_element_type=F32)
            o_ref[h, rows, :] = (_silu(g_ref[h, rows, :]) * _rms(o, nw_ref[h])).astype(o_ref.dtype)


def _retention_tables(S):
    half = LANES // 2
    inv_freq = ROPE_BASE ** (-jnp.arange(half, dtype=F32) / half)
    ang = jnp.arange(S, dtype=jnp.int32).astype(F32)[:, None] * inv_freq[None, :]
    cos, sin = jnp.cos(ang), jnp.sin(ang)
    cosf = jnp.concatenate([cos, cos], axis=-1)
    sinf = jnp.concatenate([-sin, sin], axis=-1)
    log_g = jnp.log1p(-jnp.exp2(-5.0 - jnp.arange(N_RET, dtype=F32)))
    idx = jnp.arange(CHUNK, dtype=F32)
    diff = idx[:, None] - idx[None, :]
    dec = jnp.where(diff[None] >= 0, jnp.exp(jnp.maximum(diff, 0.0)[None] * log_g[:, None, None]), 0.0)
    xi = jnp.exp((idx[None, :] + 1.0) * log_g[:, None])
    zeta = jnp.exp((CHUNK - 1.0 - idx)[None, :] * log_g[:, None])
    cd = jnp.exp(CHUNK * log_g)
    bcast = lambda t: jnp.broadcast_to(t[:, :, None], (N_RET, CHUNK, LANES))
    return cosf, sinf, dec, bcast(xi), bcast(zeta), jnp.broadcast_to(cd[:, None, None], (N_RET, 1, LANES))


def _retention(proj, tables, norm_w, tr=1024, nh=N_RET):
    S = proj.shape[1]
    cosf, sinf, dec, xi, zeta, cd = tables
    assert all(off % nh == 0 for off in (RQ, RK, RV, RG)) and N_RET % nh == 0
    slab = lambda off: pl.BlockSpec((nh, tr, LANES), lambda h, i: (off // nh + h, i, 0))
    tab = pl.BlockSpec((tr, LANES), lambda h, i: (i, 0))
    per_head = lambda r: pl.BlockSpec((nh, r, LANES), lambda h, i: (h, 0, 0))
    return pl.pallas_call(
        _retention_kernel,
        grid=(N_RET // nh, S // tr),
        in_specs=[slab(RQ), slab(RK), slab(RV), slab(RG), tab, tab,
                  per_head(CHUNK), per_head(CHUNK), per_head(CHUNK), per_head(1), per_head(1)],
        out_specs=pl.BlockSpec((nh, tr, LANES), lambda h, i: (h, i, 0)),
        out_shape=jax.ShapeDtypeStruct((N_RET, S, LANES), BF16),
        scratch_shapes=[pltpu.VMEM((nh, LANES, LANES), F32),
                        pltpu.VMEM((nh * tr // CHUNK, CHUNK, CHUNK + LANES), BF16),
                        pltpu.VMEM((nh * tr // CHUNK, LANES, LANES), F32),
                        pltpu.VMEM((nh * tr // CHUNK, LANES, LANES), BF16)],
        compiler_params=_params("parallel", "arbitrary"),
        name="retention",
    )(proj, proj, proj, proj, cosf, sinf, dec, xi, zeta, cd, norm_w.reshape(N_RET, 1, LANES))


D2 = DILATIONS[1]
TROWS = SPAN // D2
UNITS = 3 * (SPAN // CHUNK)
assert DILATIONS == (1, D2, D2 * D2) and SPAN == CHUNK * D2 * D2


def _dilated_kernel(q_ref, k_ref, v_ref, o_ref, tq, tk, tv, kt, vt, sbuf, o1, d1, m1, o2, d2, m2, o3, d3, m3):
    n = pl.program_id(1)
    slot = n % 2
    pslot = 1 - slot
    nblk = SPAN // CHUNK
    scale = LANES ** -0.5
    blk = lambda i: pl.ds(i * CHUNK, CHUNK)
    strided = lambda start: pl.ds(start, CHUNK, stride=D2)

    @pl.when(n == 0)
    def _():
        tk[1] = jnp.zeros(tk.shape[1:], F32)
        tv[1] = jnp.zeros(tv.shape[1:], F32)
        kt[...] = jnp.zeros(kt.shape, F32)
        vt[...] = jnp.zeros(vt.shape, F32)

    for r in range(D2):
        tq[r] = q_ref[pl.ds(r, TROWS, stride=D2), :]
        tk[slot, r] = k_ref[pl.ds(r, TROWS, stride=D2), :]
        tv[slot, r] = v_ref[pl.ds(r, TROWS, stride=D2), :]

    qi = lax.broadcasted_iota(jnp.int32, (CHUNK, CHUNK), 0)
    ci = lax.broadcasted_iota(jnp.int32, (CHUNK, CHUNK), 1)
    own_ok = ci <= qi
    prev_ok = ci >= qi
    prev_ok_span = ci >= qi + jnp.where(n > 0, 0, CHUNK)

    def units(it):
        ss, r = it // D2, it % D2
        narrow_prev = (lambda kv: kv[2][...]) if it == 0 else (lambda kv: kv[0][blk(it - 1), :])
        mid_prev = ((lambda kv: kv[1][slot, r, blk(ss - 1), :]) if ss > 0 else
                    (lambda kv: kv[1][pslot, r, blk(D2 - 1), :]))
        return (
            (lambda: q_ref[blk(it), :], narrow_prev, lambda kv: kv[0][blk(it), :],
             prev_ok if it > 0 else prev_ok_span, (blk(it), slice(None))),
            (lambda: tq[r, blk(ss), :], mid_prev, lambda kv: kv[1][slot, r, blk(ss), :],
             prev_ok if ss > 0 else prev_ok_span, (r, blk(ss), slice(None))),
            (lambda: tq[r, strided(ss), :], lambda kv: kv[1][pslot, r, strided(ss), :],
             lambda kv: kv[1][slot, r, strided(ss), :], prev_ok_span, (r, strided(ss), slice(None))),
        )

    keys = (k_ref, tk, kt)
    vals = (v_ref, tv, vt)

    for it in range(nblk):
        for p, (load_q, load_prev, load_own, mask_prev, rows), m_s in zip(range(3), units(it), (m1, m2, m3)):
            k = jnp.concatenate([load_prev(keys), load_own(keys)], axis=0).astype(BF16)
            s = lax.dot_general(load_q().astype(BF16), k, (((1,), (1,)), ((), ())),
                                preferred_element_type=F32)
            sp = jnp.where(mask_prev, s[:, :CHUNK], NEG_INF)
            so = jnp.where(own_ok, s[:, CHUNK:], NEG_INF)
            sbuf[3 * it + p, :, 0:CHUNK] = sp
            sbuf[3 * it + p, :, CHUNK:2 * CHUNK] = so
            m_s[rows] = jnp.broadcast_to(jnp.max(jnp.maximum(sp, so), axis=-1, keepdims=True), (CHUNK, LANES))

    ones = jnp.ones((2 * CHUNK, LANES), BF16)
    for it in range(nblk):
        for p, (_, load_prev, load_own, _, rows), o_s, d_s, m_s in zip(
                range(3), units(it), (o1, o2, o3), (d1, d2, d3), (m1, m2, m3)):
            m = m_s[rows]
            e = jnp.concatenate([jnp.exp((sbuf[3 * it + p, :, 0:CHUNK] - m) * scale),
                                 jnp.exp((sbuf[3 * it + p, :, CHUNK:2 * CHUNK] - m) * scale)], axis=1)
            v = jnp.concatenate([load_prev(vals), load_own(vals)], axis=0).astype(BF16)
            oa = jnp.dot(e.astype(BF16), jnp.concatenate([v, ones], axis=1), preferred_element_type=F32)
            o_s[rows] = oa[:, :LANES]
            d_s[rows] = oa[:, LANES:]

    kt[...] = k_ref[blk(nblk - 1), :]
    vt[...] = v_ref[blk(nblk - 1), :]

    def phase_c(j, carry):
        for r in range(D2):
            nat = (pl.ds(j * CHUNK * D2 + r, CHUNK, stride=D2), slice(None))
            trw = (r, pl.ds(j * CHUNK, CHUNK), slice(None))
            ma, mb, mc = m1[nat], m2[trw], m3[trw]
            mx = jnp.maximum(jnp.maximum(ma, mb), mc)
            wa, wb, wc = (jnp.exp((t - mx) * scale) for t in (ma, mb, mc))
            num = wa * o1[nat] + wb * o2[trw] + wc * o3[trw]
            den = wa * d1[nat] + wb * d2[trw] + wc * d3[trw]
            o_ref[nat] = num / den
        return carry

    lax.fori_loop(0, TROWS // CHUNK, phase_c, 0)


def _dilated_attention(proj):
    S = proj.shape[1]
    slab = lambda off: pl.BlockSpec((None, SPAN, LANES), lambda h, n: (off + h, n, 0))
    nat = pltpu.VMEM((SPAN, LANES), F32)
    res = pltpu.VMEM((D2, TROWS, LANES), F32)
    ring = pltpu.VMEM((2, D2, TROWS, LANES), F32)
    tail = pltpu.VMEM((CHUNK, LANES), F32)
    return pl.pallas_call(
        _dilated_kernel,
        grid=(N_ATT, S // SPAN),
        in_specs=[slab(AQ), slab(AK), slab(AV)],
        out_specs=pl.BlockSpec((None, SPAN, LANES), lambda h, n: (h, n, 0)),
        out_shape=jax.ShapeDtypeStruct((N_ATT, S, LANES), F32),
        scratch_shapes=[res, ring, ring, tail, tail, pltpu.VMEM((UNITS, CHUNK, 2 * CHUNK), F32),
                        nat, nat, nat, res, res, res, res, res, res],
        compiler_params=_params("parallel", "arbitrary"),
        name="dilated_attention",
    )(proj, proj, proj)


def _gmlp_kernel(*refs):
    G = N_GMLP
    u_refs, v_refs = refs[:G], refs[G:2 * G]
    lnw_ref, ws_ref, bs_ref, ow_ref, o_ref = refs[2 * G:]
    tm = o_ref.shape[1]
    width = G * LANES
    gv = [jax.nn.gelu(r[...]) for r in v_refs]
    mu = sum(jnp.sum(t, axis=-1, keepdims=True) for t in gv) / width
    xc = [t - mu for t in gv]
    inv = lax.rsqrt(sum(jnp.sum(t * t, axis=-1, keepdims=True) for t in xc) / width + EPS)
    qi = lax.broadcasted_iota(jnp.int32, (CHUNK, CHUNK), 0)
    ki = lax.broadcasted_iota(jnp.int32, (CHUNK, CHUNK), 1)
    outs = []
    for g in range(G):
        vn = (xc[g] * inv * lnw_ref[g]).astype(BF16)
        wm = jnp.where(ki <= qi, ws_ref[g], 0.0).astype(BF16)
        sp = [jnp.dot(wm, vn[c * CHUNK:(c + 1) * CHUNK, :], preferred_element_type=F32) + bs_ref[g]
              for c in range(tm // CHUNK)]
        outs.append(jax.nn.gelu(u_refs[g][...]) * jnp.concatenate(sp, axis=0))
    inv_o = lax.rsqrt(sum(jnp.sum(t * t, axis=-1, keepdims=True) for t in outs) / width + EPS)
    for g in range(G):
        o_ref[g] = (outs[g] * inv_o * ow_ref[g]).astype(o_ref.dtype)


def _gmlp(proj, ln_w, ws, bs, out_w, tm=512):
    S = proj.shape[1]
    G = N_GMLP
    slab = lambda s: pl.BlockSpec((None, tm, LANES), lambda i: (s, i, 0))
    small = lambda shape: pl.BlockSpec(shape, lambda i: (0,) * len(shape))
    return pl.pallas_call(
        _gmlp_kernel,
        grid=(S // tm,),
        in_specs=[slab(GU + g) for g in range(G)] + [slab(GV + g) for g in range(G)]
        + [small((G, 1, LANES)), small((G, CHUNK, CHUNK)), small((G, CHUNK, 1)), small((G, 1, LANES))],
        out_specs=pl.BlockSpec((G, tm, LANES), lambda i: (0, i, 0)),
        out_shape=jax.ShapeDtypeStruct((G, S, LANES), BF16),
        compiler_params=_params("parallel"),
        name="gmlp",
    )(*([proj] * (2 * G)), ln_w.reshape(G, 1, LANES), ws, bs.reshape(G, CHUNK, 1), out_w.reshape(G, 1, LANES))


def _outproj_kernel(ret_ref, att_ref, gm_ref, x_ref, attw_ref, wb, n2w_ref, x1_ref, h2_ref, a_scr, *, nc):
    col = lambda j: slice(j * LANES, (j + 1) * LANES)
    for j in range(N_RET):
        a_scr[:, col(j)] = ret_ref[j].astype(BF16)
    att = [att_ref[j] for j in range(N_ATT)]
    inv = lax.rsqrt(sum(jnp.sum(t * t, axis=-1, keepdims=True) for t in att) / (N_ATT * LANES) + EPS)
    for j in range(N_ATT):
        a_scr[:, col(N_RET + j)] = (att[j] * inv * attw_ref[j]).astype(BF16)
    for j in range(N_GMLP):
        a_scr[:, col(N_RET + N_ATT + j)] = gm_ref[j].astype(BF16)
    D = x_ref.shape[1]
    ssq = 0.0
    for c in range(D // nc):
        cs = slice(c * nc, (c + 1) * nc)
        x1 = x_ref[:, cs] + jnp.dot(a_scr[...], wb[:, cs], preferred_element_type=F32)
        x1_ref[:, cs] = x1
        ssq = ssq + jnp.sum(x1 * x1, axis=-1, keepdims=True)
    h2_ref[...] = (x1_ref[...] * lax.rsqrt(ssq / D + EPS) * n2w_ref[...]).astype(BF16)


def _outproj(ret, att, gm, x, att_w, w, n2w, tm=512, nc=512):
    S, D = x.shape
    K = w.shape[0]
    rows = lambda n: pl.BlockSpec((n, tm, LANES), lambda i: (0, i, 0))
    full = lambda: pl.BlockSpec((tm, D), lambda i: (i, 0))
    const = lambda shape: pl.BlockSpec(shape, lambda i: (0,) * len(shape))
    return pl.pallas_call(
        functools.partial(_outproj_kernel, nc=nc),
        grid=(S // tm,),
        in_specs=[rows(N_RET), rows(N_ATT), rows(N_GMLP), full(), const((N_ATT, 1, LANES)),
                  pl.BlockSpec((K, D), lambda i: (0, 0), pipeline_mode=pl.Buffered(1)), const((1, D))],
        out_specs=[full(), full()],
        out_shape=[jax.ShapeDtypeStruct((S, D), F32), jax.ShapeDtypeStruct((S, D), BF16)],
        scratch_shapes=[pltpu.VMEM((tm, K), BF16)],
        compiler_params=_params("parallel"),
        name="outproj",
    )(ret, att, gm, x, att_w.reshape(N_ATT, 1, LANES), w, n2w.reshape(1, D))


def _upproj_kernel(a_ref, wg_ref, wv_ref, cwg_ref, cwv_ref, cbg_ref, cbv_ref, wc_ref, o_ref, wcb_ref, wgb, wvb, hg, hv):
    wcb_ref[...] = wc_ref[...].astype(BF16)
    tm = a_ref.shape[0]
    nslab = hg.shape[0]
    head = SUBLANES

    @pl.when(pl.program_id(1) == 0)
    def _():
        wgb[...] = wg_ref[...].astype(BF16)
        wvb[...] = wv_ref[...].astype(BF16)
        hg[:, 0:head, :] = jnp.zeros((nslab, head, LANES), F32)
        hv[:, 0:head, :] = jnp.zeros((nslab, head, LANES), F32)

    a = a_ref[...]

    def conv(wb, cw_ref, cb_ref, hbuf):
        h = jnp.dot(a, wb[...], preferred_element_type=F32)
        outs = []
        for s in range(nslab):
            cs = slice(s * LANES, (s + 1) * LANES)
            hs = h[:, cs]
            hbuf[s, head:head + tm, :] = hs
            outs.append(cb_ref[:, cs] + cw_ref[0:1, cs] * hbuf[s, head - 2:head - 2 + tm, :]
                        + cw_ref[1:2, cs] * hbuf[s, head - 1:head - 1 + tm, :] + cw_ref[2:3, cs] * hs)
            hbuf[s, 0:head, :] = hbuf[s, tm:tm + head, :]
        return outs

    gate = conv(wgb, cwg_ref, cbg_ref, hg)
    val = conv(wvb, cwv_ref, cbv_ref, hv)
    for s in range(nslab):
        o_ref[:, s * LANES:(s + 1) * LANES] = (_silu(gate[s]) * val[s]).astype(o_ref.dtype)


def _upproj(h2, w_up, conv_w, conv_b, wc, l, tm=1024, tn=512):
    S, D = h2.shape
    F = w_up.shape[2] // 2
    nj, nm = F // tn, S // tm
    taps = conv_w.shape[1]
    hbuf = pltpu.VMEM((tn // LANES, tm + SUBLANES, LANES), F32)
    wc_in, wc_out, wc_shape = _sidecast(wc, l, nj * nm, lambda j, i: j * nm + i)
    return pl.pallas_call(
        _upproj_kernel,
        grid=(nj, nm),
        in_specs=[pl.BlockSpec((tm, D), lambda j, i: (i, 0)),
                  pl.BlockSpec((None, D, tn), lambda j, i: (l, 0, j)),
                  pl.BlockSpec((None, D, tn), lambda j, i: (l, 0, j + nj)),
                  pl.BlockSpec((None, taps, tn), lambda j, i: (l, 0, j)),
                  pl.BlockSpec((None, taps, tn), lambda j, i: (l, 0, j + nj)),
                  pl.BlockSpec((None, 1, tn), lambda j, i: (l, 0, j)),
                  pl.BlockSpec((None, 1, tn), lambda j, i: (l, 0, j + nj)),
                  wc_in],
        out_specs=[pl.BlockSpec((tm, tn), lambda j, i: (i, j)), wc_out],
        out_shape=[jax.ShapeDtypeStruct((S, F), BF16), wc_shape],
        scratch_shapes=[pltpu.VMEM((D, tn), BF16), pltpu.VMEM((D, tn), BF16), hbuf, hbuf],
        compiler_params=_params("parallel", "arbitrary"),
        name="upproj",
    )(h2, w_up, w_up, conv_w, conv_w, conv_b[:, None, :], conv_b[:, None, :], wc)


def _downproj_kernel(a_ref, w_ref, x1_ref, *refs):
    x2 = x1_ref[...] + jnp.dot(a_ref[...], w_ref[...], preferred_element_type=F32)
    if len(refs) == 1:
        refs[0][...] = x2
        return
    nw_ref, wc_ref, o_ref, xw_ref, ssq_ref, wcb_ref = refs
    o_ref[...] = x2
    xw_ref[...] = (x2 * nw_ref[...]).astype(BF16)
    ssq_ref[...] = jnp.sum(x2 * x2, axis=-1, keepdims=True)
    wcb_ref[...] = wc_ref[...].astype(BF16)


def _downproj(g, w, x1, nxt=None, tm=512, tn=1024):
    S, F = g.shape
    D = w.shape[1]
    nm = S // tm
    tile = lambda: pl.BlockSpec((tm, tn), lambda j, i: (i, j))
    in_specs = [pl.BlockSpec((tm, F), lambda j, i: (i, 0)), pl.BlockSpec((F, tn), lambda j, i: (0, j)), tile()]
    if nxt is None:
        extra, out_specs, out_shape = (), tile(), jax.ShapeDtypeStruct((S, D), F32)
    else:
        next_w, wc, lc = nxt
        wc_in, wc_out, wc_shape = _sidecast(wc, lc, (D // tn) * nm, lambda j, i: j * nm + i)
        extra = (next_w.reshape(1, D), wc)
        in_specs += [pl.BlockSpec((1, tn), lambda j, i: (0, j)), wc_in]
        out_specs = [tile(), tile(), pl.BlockSpec((None, tm, 1), lambda j, i: (j, i, 0)), wc_out]
        out_shape = [jax.ShapeDtypeStruct((S, D), F32), jax.ShapeDtypeStruct((S, D), BF16),
                     jax.ShapeDtypeStruct((D // tn, S, 1), F32), wc_shape]
    return pl.pallas_call(
        _downproj_kernel,
        grid=(D // tn, S // tm),
        in_specs=in_specs,
        out_specs=out_specs,
        out_shape=out_shape,
        compiler_params=_params("parallel", "parallel"),
        name="downproj",
    )(g, w, x1, *extra)


def kernel(x, norm1_w, w_in, ret_norm_w, att_norm_w, gmlp_ln_w, gmlp_ws, gmlp_bs, gmlp_out_w, w_out, norm2_w, w_up, conv_w, conv_b, w_down, final_norm_w):
    B, S, D = x.shape
    depth = w_in.shape[0]
    assert B == 1 and S % SPAN == 0 and w_in.shape[2] == N_SLABS * LANES
    tables = _retention_tables(S)
    xs = x.reshape(S, D)
    xw, ssq, w_out_b = _prescale(xs, norm1_w[0], w_out, 0)
    for l in range(depth):
        proj = _inproj(xw, ssq, w_in, l)
        ret = _retention(proj, tables, ret_norm_w[l])
        att = _dilated_attention(proj)
        gm = _gmlp(proj, gmlp_ln_w[l], gmlp_ws[l], gmlp_bs[l], gmlp_out_w[l])
        x1, h2 = _outproj(ret, att, gm, xs, att_norm_w[l], w_out_b, norm2_w[l])
        g, w_down_b = _upproj(h2, w_up, conv_w, conv_b, w_down, l)
        if l + 1 < depth:
            xs, xw, ssq, w_out_b = _downproj(g, w_down_b, x1, (norm1_w[l + 1], w_out, l + 1))
        else:
            xs = _downproj(g, w_down_b, x1)
    return _rmsnorm(xs, final_norm_w).reshape(B, S, D)
```

```python
import functools

import jax
import jax.numpy as jnp
from jax import lax
from jax.experimental import pallas as pl
from jax.experimental.pallas import tpu as pltpu

F32 = jnp.float32
BF16 = jnp.bfloat16

LANES = 128
SUBLANES = 8
BF16_SUBLANES = 16
CHUNK = 128
N_RET = 6
N_ATT = 6
N_GMLP = 4
DILATIONS = (1, 4, 16)
SPAN = CHUNK * DILATIONS[-1]
ROPE_BASE = 10000.0
EPS = 1e-6
NEG_INF = -1e30
VMEM_LIMIT = 56 * 1024 * 1024

RQ, RK, RV, RG = 0, N_RET, 2 * N_RET, 3 * N_RET
AQ = 4 * N_RET
AK = AQ + N_ATT
AV = AK + N_ATT
GU = AV + N_ATT
GV = GU + N_GMLP
N_SLABS = GV + N_GMLP


def _params(*sem):
    return pltpu.CompilerParams(dimension_semantics=sem, vmem_limit_bytes=VMEM_LIMIT)


def _sidecast(w, l, steps, step_of):
    _, rows, cols = w.shape
    band = rows // steps
    assert band * steps == rows and band % BF16_SUBLANES == 0
    return (pl.BlockSpec((None, band, cols), lambda *g: (l, step_of(*g), 0)),
            pl.BlockSpec((band, cols), lambda *g: (step_of(*g), 0)),
            jax.ShapeDtypeStruct((rows, cols), BF16))


def _rms(x, w):
    return x * lax.rsqrt(jnp.mean(x * x, axis=-1, keepdims=True) + EPS) * w


def _silu(x):
    hx = 0.5 * x
    return hx + hx * jnp.tanh(hx)


def _prescale_kernel(x_ref, w_ref, wc_ref, xw_ref, ssq_ref, wcb_ref):
    x = x_ref[...]
    xw_ref[...] = (x * w_ref[...]).astype(BF16)
    ssq_ref[...] = jnp.sum(x * x, axis=-1, keepdims=True)
    wcb_ref[...] = wc_ref[...].astype(BF16)


def _prescale(x, w, wc, lc, tm=512):
    S, D = x.shape
    wc_in, wc_out, wc_shape = _sidecast(wc, lc, S // tm, lambda i: i)
    return pl.pallas_call(
        _prescale_kernel,
        grid=(S // tm,),
        in_specs=[pl.BlockSpec((tm, D), lambda i: (i, 0)), pl.BlockSpec((1, D), lambda i: (0, 0)), wc_in],
        out_specs=[pl.BlockSpec((tm, D), lambda i: (i, 0)), pl.BlockSpec((None, tm, 1), lambda i: (0, i, 0)), wc_out],
        out_shape=[jax.ShapeDtypeStruct((S, D), BF16), jax.ShapeDtypeStruct((1, S, 1), F32), wc_shape],
        compiler_params=_params("parallel"),
        name="prescale",
    )(x, w.reshape(1, D), wc)


def _inproj_kernel(a_ref, ssq_ref, w_ref, o_ref, wb):
    @pl.when(pl.program_id(1) == 0)
    def _():
        wb[...] = w_ref[...].astype(BF16)

    D = a_ref.shape[1]
    r = lax.rsqrt(sum(ssq_ref[p] for p in range(ssq_ref.shape[0])) / D + EPS)
    acc = jnp.dot(a_ref[...], wb[...], preferred_element_type=F32)
    for j in range(o_ref.shape[0]):
        o_ref[j] = acc[:, j * LANES:(j + 1) * LANES] * r


def _inproj(xw, ssq, w, l, tm=1024, tn=1280):
    S, D = xw.shape
    N = w.shape[2]
    parts = ssq.shape[0]
    return pl.pallas_call(
        _inproj_kernel,
        grid=(N // tn, S // tm),
        in_specs=[pl.BlockSpec((tm, D), lambda j, i: (i, 0)), pl.BlockSpec((parts, tm, 1), lambda j, i: (0, i, 0)),
                  pl.BlockSpec((None, D, tn), lambda j, i: (l, 0, j))],
        out_specs=pl.BlockSpec((tn // LANES, tm, LANES), lambda j, i: (j, i, 0)),
        out_shape=jax.ShapeDtypeStruct((N // LANES, S, LANES), F32),
        scratch_shapes=[pltpu.VMEM((D, tn), BF16)],
        compiler_params=_params("parallel", "arbitrary"),
        name="inproj",
    )(xw, ssq, w)


def _retention_kernel(q_ref, k_ref, v_ref, g_ref, cos_ref, sin_ref, dec_ref, xi_ref, zeta_ref, cd_ref,
                      nw_ref, o_ref, state_ref, lbuf, ubuf, sbuf):
    @pl.when(pl.program_id(1) == 0)
    def _():
        state_ref[...] = jnp.zeros_like(state_ref)

    scale = LANES ** -0.5
    nh, tr = q_ref.shape[0], q_ref.shape[1]
    nc = tr // CHUNK
    for c in range(nc):
        rows = pl.ds(c * CHUNK, CHUNK)
        cos, sin = cos_ref[rows, :], sin_ref[rows, :]
        for h in range(nh):
            u = c * nh + h
            q, k = q_ref[h, rows, :], k_ref[h, rows, :]
            qr = q * cos + pltpu.roll(q, LANES // 2, 1) * sin
            kr = (k * cos + pltpu.roll(k, LANES // 2, 1) * sin) * scale
            vb = v_ref[h, rows, :].astype(BF16)
            s = lax.dot_general(qr.astype(BF16), kr.astype(BF16), (((1,), (1,)), ((), ())),
                                preferred_element_type=F32) * dec_ref[h]
            lbuf[u, :, 0:CHUNK] = s.astype(BF16)
            lbuf[u, :, CHUNK:CHUNK + LANES] = (qr * xi_ref[h]).astype(BF16)
            ubuf[u] = lax.dot_general((kr * zeta_ref[h]).astype(BF16), vb, (((0,), (0,)), ((), ())),
                                      preferred_element_type=F32)
    for h in range(nh):
        state = state_ref[h]
        for c in range(nc):
            sbuf[c * nh + h] = state.astype(BF16)
            state = cd_ref[h] * state + ubuf[c * nh + h]
        state_ref[h] = state
    for c in range(nc):
        rows = pl.ds(c * CHUNK, CHUNK)
        for h in range(nh):
            u = c * nh + h
            rhs = jnp.concatenate([v_ref[h, rows, :].astype(BF16), sbuf[u]], axis=0)
            o = jnp.dot(lbuf[u], rhs, preferred_element_type=F32)
            o_ref[h, rows, :] = (_silu(g_ref[h, rows, :]) * _rms(o, nw_ref[h])).astype(o_ref.dtype)


def _retention_tables(S):
    half = LANES // 2
    inv_freq = ROPE_BASE ** (-jnp.arange(half, dtype=F32) / half)
    ang = jnp.arange(S, dtype=jnp.int32).astype(F32)[:, None] * inv_freq[None, :]
    cos, sin = jnp.cos(ang), jnp.sin(ang)
    cosf = jnp.concatenate([cos, cos], axis=-1)
    sinf = jnp.concatenate([-sin, sin], axis=-1)
    log_g = jnp.log1p(-jnp.exp2(-5.0 - jnp.arange(N_RET, dtype=F32)))
    idx = jnp.arange(CHUNK, dtype=F32)
    diff = idx[:, None] - idx[None, :]
    dec = jnp.where(diff[None] >= 0, jnp.exp(jnp.maximum(diff, 0.0)[None] * log_g[:, None, None]), 0.0)
    xi = jnp.exp((idx[None, :] + 1.0) * log_g[:, None])
    zeta = jnp.exp((CHUNK - 1.0 - idx)[None, :] * log_g[:, None])
    cd = jnp.exp(CHUNK * log_g)
    bcast = lambda t: jnp.broadcast_to(t[:, :, None], (N_RET, CHUNK, LANES))
    return cosf, sinf, dec, bcast(xi), bcast(zeta), jnp.broadcast_to(cd[:, None, None], (N_RET, 1, LANES))


def _retention(proj, tables, norm_w, tr=1024, nh=N_RET):
    S = proj.shape[1]
    cosf, sinf, dec, xi, zeta, cd = tables
    assert all(off % nh == 0 for off in (RQ, RK, RV, RG)) and N_RET % nh == 0
    slab = lambda off: pl.BlockSpec((nh, tr, LANES), lambda h, i: (off // nh + h, i, 0))
    tab = pl.BlockSpec((tr, LANES), lambda h, i: (i, 0))
    per_head = lambda r: pl.BlockSpec((nh, r, LANES), lambda h, i: (h, 0, 0))
    return pl.pallas_call(
        _retention_kernel,
        grid=(N_RET // nh, S // tr),
        in_specs=[slab(RQ), slab(RK), slab(RV), slab(RG), tab, tab,
                  per_head(CHUNK), per_head(CHUNK), per_head(CHUNK), per_head(1), per_head(1)],
        out_specs=pl.BlockSpec((nh, tr, LANES), lambda h, i: (h, i, 0)),
        out_shape=jax.ShapeDtypeStruct((N_RET, S, LANES), BF16),
        scratch_shapes=[pltpu.VMEM((nh, LANES, LANES), F32),
                        pltpu.VMEM((nh * tr // CHUNK, CHUNK, CHUNK + LANES), BF16),
                        pltpu.VMEM((nh * tr // CHUNK, LANES, LANES), F32),
                        pltpu.VMEM((nh * tr // CHUNK, LANES, LANES), BF16)],
        compiler_params=_params("parallel", "arbitrary"),
        name="retention",
    )(proj, proj, proj, proj, cosf, sinf, dec, xi, zeta, cd, norm_w.reshape(N_RET, 1, LANES))


D2 = DILATIONS[1]
TROWS = SPAN // D2
UNITS = 3 * (SPAN // CHUNK)
assert DILATIONS == (1, D2, D2 * D2) and SPAN == CHUNK * D2 * D2


def _dilated_kernel(q_ref, k_ref, v_ref, o_ref, tq, tk, tv, kt, vt, sbuf, o1, d1, m1, o2, d2, m2, o3, d3, m3):
    n = pl.program_id(1)
    slot = n % 2
    pslot = 1 - slot
    nblk = SPAN // CHUNK
    scale = LANES ** -0.5
    blk = lambda i: pl.ds(i * CHUNK, CHUNK)
    strided = lambda start: pl.ds(start, CHUNK, stride=D2)

    @pl.when(n == 0)
    def _():
        tk[1] = jnp.zeros(tk.shape[1:], F32)
        tv[1] = jnp.zeros(tv.shape[1:], F32)
        kt[...] = jnp.zeros(kt.shape, F32)
        vt[...] = jnp.zeros(vt.shape, F32)

    for r in range(D2):
        tq[r] = q_ref[pl.ds(r, TROWS, stride=D2), :]
        tk[slot, r] = k_ref[pl.ds(r, TROWS, stride=D2), :]
        tv[slot, r] = v_ref[pl.ds(r, TROWS, stride=D2), :]

    qi = lax.broadcasted_iota(jnp.int32, (CHUNK, CHUNK), 0)
    ci = lax.broadcasted_iota(jnp.int32, (CHUNK, CHUNK), 1)
    own_ok = ci <= qi
    prev_ok = ci >= qi
    prev_ok_span = ci >= qi + jnp.where(n > 0, 0, CHUNK)

    def units(it):
        ss, r = it // D2, it % D2
        narrow_prev = (lambda kv: kv[2][...]) if it == 0 else (lambda kv: kv[0][blk(it - 1), :])
        mid_prev = ((lambda kv: kv[1][slot, r, blk(ss - 1), :]) if ss > 0 else
                    (lambda kv: kv[1][pslot, r, blk(D2 - 1), :]))
        return (
            (lambda: q_ref[blk(it), :], narrow_prev, lambda kv: kv[0][blk(it), :],
             prev_ok if it > 0 else prev_ok_span, (blk(it), slice(None))),
            (lambda: tq[r, blk(ss), :], mid_prev, lambda kv: kv[1][slot, r, blk(ss), :],
             prev_ok if ss > 0 else prev_ok_span, (r, blk(ss), slice(None))),
            (lambda: tq[r, strided(ss), :], lambda kv: kv[1][pslot, r, strided(ss), :],
             lambda kv: kv[1][slot, r, strided(ss), :], prev_ok_span, (r, strided(ss), slice(None))),
        )

    keys = (k_ref, tk, kt)
    vals = (v_ref, tv, vt)

    for it in range(nblk):
        for p, (load_q, load_prev, load_own, mask_prev, rows), m_s in zip(range(3), units(it), (m1, m2, m3)):
            k = jnp.concatenate([load_prev(keys), load_own(keys)], axis=0).astype(BF16)
            s = lax.dot_general(load_q().astype(BF16), k, (((1,), (1,)), ((), ())),
                                preferred_element_type=F32)
            sp = jnp.where(mask_prev, s[:, :CHUNK], NEG_INF)
            so = jnp.where(own_ok, s[:, CHUNK:], NEG_INF)
            sbuf[3 * it + p, :, 0:CHUNK] = sp
            sbuf[3 * it + p, :, CHUNK:2 * CHUNK] = so
            m_s[rows] = jnp.broadcast_to(jnp.max(jnp.maximum(sp, so), axis=-1, keepdims=True), (CHUNK, LANES))

    ones = jnp.ones((2 * CHUNK, LANES), BF16)
    for it in range(nblk):
        for p, (_, load_prev, load_own, _, rows), o_s, d_s, m_s in zip(
                range(3), units(it), (o1, o2, o3), (d1, d2, d3), (m1, m2, m3)):
            m = m_s[rows]
            e = jnp.concatenate([jnp.exp((sbuf[3 * it + p, :, 0:CHUNK] - m) * scale),
                                 jnp.exp((sbuf[3 * it + p, :, CHUNK:2 * CHUNK] - m) * scale)], axis=1)
            v = jnp.concatenate([load_prev(vals), load_own(vals)], axis=0).astype(BF16)
            oa = jnp.dot(e.astype(BF16), jnp.concatenate([v, ones], axis=1), preferred_element_type=F32)
            o_s[rows] = oa[:, :LANES]
            d_s[rows] = oa[:, LANES:]

    kt[...] = k_ref[blk(nblk - 1), :]
    vt[...] = v_ref[blk(nblk - 1), :]

    def phase_c(j, carry):
        for r in range(D2):
            nat = (pl.ds(j * CHUNK * D2 + r, CHUNK, stride=D2), slice(None))
            trw = (r, pl.ds(j * CHUNK, CHUNK), slice(None))
            ma, mb, mc = m1[nat], m2[trw], m3[trw]
            mx = jnp.maximum(jnp.maximum(ma, mb), mc)
            wa, wb, wc = (jnp.exp((t - mx) * scale) for t in (ma, mb, mc))
            num = wa * o1[nat] + wb * o2[trw] + wc * o3[trw]
            den = wa * d1[nat] + wb * d2[trw] + wc * d3[trw]
            o_ref[nat] = num / den
        return carry

    lax.fori_loop(0, TROWS // CHUNK, phase_c, 0)


def _dilated_attention(proj):
    S = proj.shape[1]
    slab = lambda off: pl.BlockSpec((None, SPAN, LANES), lambda h, n: (off + h, n, 0))
    nat = pltpu.VMEM((SPAN, LANES), F32)
    res = pltpu.VMEM((D2, TROWS, LANES), F32)
    ring = pltpu.VMEM((2, D2, TROWS, LANES), F32)
    tail = pltpu.VMEM((CHUNK, LANES), F32)
    return pl.pallas_call(
        _dilated_kernel,
        grid=(N_ATT, S // SPAN),
        in_specs=[slab(AQ), slab(AK), slab(AV)],
        out_specs=pl.BlockSpec((None, SPAN, LANES), lambda h, n: (h, n, 0)),
        out_shape=jax.ShapeDtypeStruct((N_ATT, S, LANES), F32),
        scratch_shapes=[res, ring, ring, tail, tail, pltpu.VMEM((UNITS, CHUNK, 2 * CHUNK), F32),
                        nat, nat, nat, res, res, res, res, res, res],
        compiler_params=_params("parallel", "arbitrary"),
        name="dilated_attention",
    )(proj, proj, proj)


def _gmlp_kernel(*refs):
    G = N_GMLP
    u_refs, v_refs = refs[:G], refs[G:2 * G]
    lnw_ref, ws_ref, bs_ref, ow_ref, o_ref = refs[2 * G:]
    tm = o_ref.shape[1]
    width = G * LANES
    gv = [jax.nn.gelu(r[...]) for r in v_refs]
    mu = sum(jnp.sum(t, axis=-1, keepdims=True) for t in gv) / width
    xc = [t - mu for t in gv]
    inv = lax.rsqrt(sum(jnp.sum(t * t, axis=-1, keepdims=True) for t in xc) / width + EPS)
    qi = lax.broadcasted_iota(jnp.int32, (CHUNK, CHUNK), 0)
    ki = lax.broadcasted_iota(jnp.int32, (CHUNK, CHUNK), 1)
    outs = []
    for g in range(G):
        vn = (xc[g] * inv * lnw_ref[g]).astype(BF16)
        wm = jnp.where(ki <= qi, ws_ref[g], 0.0).astype(BF16)
        sp = [jnp.dot(wm, vn[c * CHUNK:(c + 1) * CHUNK, :], preferred_element_type=F32) + bs_ref[g]
              for c in range(tm // CHUNK)]
        outs.append(jax.nn.gelu(u_refs[g][...]) * jnp.concatenate(sp, axis=0))
    inv_o = lax.rsqrt(sum(jnp.sum(t * t, axis=-1, keepdims=True) for t in outs) / width + EPS)
    for g in range(G):
        o_ref[g] = (outs[g] * inv_o * ow_ref[g]).astype(o_ref.dtype)


def _gmlp(proj, ln_w, ws, bs, out_w, tm=1024):
    S = proj.shape[1]
    G = N_GMLP
    slab = lambda s: pl.BlockSpec((None, tm, LANES), lambda i: (s, i, 0))
    small = lambda shape: pl.BlockSpec(shape, lambda i: (0,) * len(shape))
    return pl.pallas_call(
        _gmlp_kernel,
        grid=(S // tm,),
        in_specs=[slab(GU + g) for g in range(G)] + [slab(GV + g) for g in range(G)]
        + [small((G, 1, LANES)), small((G, CHUNK, CHUNK)), small((G, CHUNK, 1)), small((G, 1, LANES))],
        out_specs=pl.BlockSpec((G, tm, LANES), lambda i: (0, i, 0)),
        out_shape=jax.ShapeDtypeStruct((G, S, LANES), BF16),
        compiler_params=_params("parallel"),
        name="gmlp",
    )(*([proj] * (2 * G)), ln_w.reshape(G, 1, LANES), ws, bs.reshape(G, CHUNK, 1), out_w.reshape(G, 1, LANES))


def _outproj_kernel(ret_ref, att_ref, gm_ref, x_ref, attw_ref, wb, n2w_ref, x1_ref, h2_ref, a_scr, *, nc):
    col = lambda j: slice(j * LANES, (j + 1) * LANES)
    for j in range(N_RET):
        a_scr[:, col(j)] = ret_ref[j].astype(BF16)
    att = [att_ref[j] for j in range(N_ATT)]
    inv = lax.rsqrt(sum(jnp.sum(t * t, axis=-1, keepdims=True) for t in att) / (N_ATT * LANES) + EPS)
    for j in range(N_ATT):
        a_scr[:, col(N_RET + j)] = (att[j] * inv * attw_ref[j]).astype(BF16)
    for j in range(N_GMLP):
        a_scr[:, col(N_RET + N_ATT + j)] = gm_ref[j].astype(BF16)
    D = x_ref.shape[1]
    ssq = 0.0
    for c in range(D // nc):
        cs = slice(c * nc, (c + 1) * nc)
        x1 = x_ref[:, cs] + jnp.dot(a_scr[...], wb[:, cs], preferred_element_type=F32)
        x1_ref[:, cs] = x1
        ssq = ssq + jnp.sum(x1 * x1, axis=-1, keepdims=True)
    h2_ref[...] = (x1_ref[...] * lax.rsqrt(ssq / D + EPS) * n2w_ref[...]).astype(BF16)


def _outproj(ret, att, gm, x, att_w, w, n2w, tm=512, nc=512):
    S, D = x.shape
    K = w.shape[0]
    rows = lambda n: pl.BlockSpec((n, tm, LANES), lambda i: (0, i, 0))
    full = lambda: pl.BlockSpec((tm, D), lambda i: (i, 0))
    const = lambda shape: pl.BlockSpec(shape, lambda i: (0,) * len(shape))
    return pl.pallas_call(
        functools.partial(_outproj_kernel, nc=nc),
        grid=(S // tm,),
        in_specs=[rows(N_RET), rows(N_ATT), rows(N_GMLP), full(), const((N_ATT, 1, LANES)),
                  pl.BlockSpec((K, D), lambda i: (0, 0), pipeline_mode=pl.Buffered(1)), const((1, D))],
        out_specs=[full(), full()],
        out_shape=[jax.ShapeDtypeStruct((S, D), F32), jax.ShapeDtypeStruct((S, D), BF16)],
        scratch_shapes=[pltpu.VMEM((tm, K), BF16)],
        compiler_params=_params("parallel"),
        name="outproj",
    )(ret, att, gm, x, att_w.reshape(N_ATT, 1, LANES), w, n2w.reshape(1, D))


def _upproj_kernel(a_ref, wg_ref, wv_ref, cwg_ref, cwv_ref, cbg_ref, cbv_ref, wc_ref, o_ref, wcb_ref, wgb, wvb, hg, hv):
    wcb_ref[...] = wc_ref[...].astype(BF16)
    tm = a_ref.shape[0]
    nslab = hg.shape[0]
    head = SUBLANES

    @pl.when(pl.program_id(1) == 0)
    def _():
        wgb[...] = wg_ref[...].astype(BF16)
        wvb[...] = wv_ref[...].astype(BF16)
        hg[:, 0:head, :] = jnp.zeros((nslab, head, LANES), F32)
        hv[:, 0:head, :] = jnp.zeros((nslab, head, LANES), F32)

    a = a_ref[...]

    def conv(wb, cw_ref, cb_ref, hbuf):
        h = jnp.dot(a, wb[...], preferred_element_type=F32)
        outs = []
        for s in range(nslab):
            cs = slice(s * LANES, (s + 1) * LANES)
            hs = h[:, cs]
            hbuf[s, head:head + tm, :] = hs
            outs.append(cb_ref[:, cs] + cw_ref[0:1, cs] * hbuf[s, head - 2:head - 2 + tm, :]
                        + cw_ref[1:2, cs] * hbuf[s, head - 1:head - 1 + tm, :] + cw_ref[2:3, cs] * hs)
            hbuf[s, 0:head, :] = hbuf[s, tm:tm + head, :]
        return outs

    gate = conv(wgb, cwg_ref, cbg_ref, hg)
    val = conv(wvb, cwv_ref, cbv_ref, hv)
    for s in range(nslab):
        o_ref[:, s * LANES:(s + 1) * LANES] = (_silu(gate[s]) * val[s]).astype(o_ref.dtype)


def _upproj(h2, w_up, conv_w, conv_b, wc, l, tm=1024, tn=512):
    S, D = h2.shape
    F = w_up.shape[2] // 2
    nj, nm = F // tn, S // tm
    taps = conv_w.shape[1]
    hbuf = pltpu.VMEM((tn // LANES, tm + SUBLANES, LANES), F32)
    wc_in, wc_out, wc_shape = _sidecast(wc, l, nj * nm, lambda j, i: j * nm + i)
    return pl.pallas_call(
        _upproj_kernel,
        grid=(nj, nm),
        in_specs=[pl.BlockSpec((tm, D), lambda j, i: (i, 0)),
                  pl.BlockSpec((None, D, tn), lambda j, i: (l, 0, j)),
                  pl.BlockSpec((None, D, tn), lambda j, i: (l, 0, j + nj)),
                  pl.BlockSpec((None, taps, tn), lambda j, i: (l, 0, j)),
                  pl.BlockSpec((None, taps, tn), lambda j, i: (l, 0, j + nj)),
                  pl.BlockSpec((None, 1, tn), lambda j, i: (l, 0, j)),
                  pl.BlockSpec((None, 1, tn), lambda j, i: (l, 0, j + nj)),
                  wc_in],
        out_specs=[pl.BlockSpec((tm, tn), lambda j, i: (i, j)), wc_out],
        out_shape=[jax.ShapeDtypeStruct((S, F), BF16), wc_shape],
        scratch_shapes=[pltpu.VMEM((D, tn), BF16), pltpu.VMEM((D, tn), BF16), hbuf, hbuf],
        compiler_params=_params("parallel", "arbitrary"),
        name="upproj",
    )(h2, w_up, w_up, conv_w, conv_w, conv_b[:, None, :], conv_b[:, None, :], wc)


def _downproj_kernel(a_ref, w_ref, x1_ref, nw_ref, wc_ref, o_ref, xw_ref, ssq_ref, wcb_ref):
    x2 = x1_ref[...] + jnp.dot(a_ref[...], w_ref[...], preferred_element_type=F32)
    o_ref[...] = x2
    xw_ref[...] = (x2 * nw_ref[...]).astype(BF16)
    ssq_ref[...] = jnp.sum(x2 * x2, axis=-1, keepdims=True)
    wcb_ref[...] = wc_ref[...].astype(BF16)


def _downproj(g, w, x1, next_w, wc, lc, tm=512, tn=1024):
    S, F = g.shape
    D = w.shape[1]
    nm = S // tm
    tile = lambda: pl.BlockSpec((tm, tn), lambda j, i: (i, j))
    wc_in, wc_out, wc_shape = _sidecast(wc, lc, (D // tn) * nm, lambda j, i: j * nm + i)
    return pl.pallas_call(
        _downproj_kernel,
        grid=(D // tn, nm),
        in_specs=[pl.BlockSpec((tm, F), lambda j, i: (i, 0)), pl.BlockSpec((F, tn), lambda j, i: (0, j)), tile(),
                  pl.BlockSpec((1, tn), lambda j, i: (0, j)), wc_in],
        out_specs=[tile(), tile(), pl.BlockSpec((None, tm, 1), lambda j, i: (j, i, 0)), wc_out],
        out_shape=[jax.ShapeDtypeStruct((S, D), F32), jax.ShapeDtypeStruct((S, D), BF16),
                   jax.ShapeDtypeStruct((D // tn, S, 1), F32), wc_shape],
        compiler_params=_params("parallel", "parallel"),
        name="downproj",
    )(g, w, x1, next_w.reshape(1, D), wc)


def _downproj_final_kernel(a_ref, w_ref, x1_ref, nw_ref, o_ref, *, nc):
    D = o_ref.shape[1]
    a = a_ref[...]
    ssq = 0.0
    for c in range(D // nc):
        cs = slice(c * nc, (c + 1) * nc)
        x2 = x1_ref[:, cs] + jnp.dot(a, w_ref[:, cs], preferred_element_type=F32)
        o_ref[:, cs] = x2
        ssq = ssq + jnp.sum(x2 * x2, axis=-1, keepdims=True)
    o_ref[...] = o_ref[...] * lax.rsqrt(ssq / D + EPS) * nw_ref[...]


def _downproj_final(g, w, x1, final_w, tm=512, nc=512):
    S, F = g.shape
    D = w.shape[1]
    rows = lambda: pl.BlockSpec((tm, D), lambda i: (i, 0))
    return pl.pallas_call(
        functools.partial(_downproj_final_kernel, nc=nc),
        grid=(S // tm,),
        in_specs=[pl.BlockSpec((tm, F), lambda i: (i, 0)),
                  pl.BlockSpec((F, D), lambda i: (0, 0), pipeline_mode=pl.Buffered(1)),
                  rows(), pl.BlockSpec((1, D), lambda i: (0, 0))],
        out_specs=rows(),
        out_shape=jax.ShapeDtypeStruct((S, D), F32),
        compiler_params=_params("parallel"),
        name="downproj_final",
    )(g, w, x1, final_w.reshape(1, D))


def kernel(x, norm1_w, w_in, ret_norm_w, att_norm_w, gmlp_ln_w, gmlp_ws, gmlp_bs, gmlp_out_w, w_out, norm2_w, w_up, conv_w, conv_b, w_down, final_norm_w):
    B, S, D = x.shape
    depth = w_in.shape[0]
    assert B == 1 and S % SPAN == 0 and w_in.shape[2] == N_SLABS * LANES
    tables = _retention_tables(S)
    xs = x.reshape(S, D)
    xw, ssq, w_out_b = _prescale(xs, norm1_w[0], w_out, 0)
    for l in range(depth):
        proj = _inproj(xw, ssq, w_in, l)
        ret = _retention(proj, tables, ret_norm_w[l])
        att = _dilated_attention(proj)
        gm = _gmlp(proj, gmlp_ln_w[l], gmlp_ws[l], gmlp_bs[l], gmlp_out_w[l])
        x1, h2 = _outproj(ret, att, gm, xs, att_norm_w[l], w_out_b, norm2_w[l])
        g, w_down_b = _upproj(h2, w_up, conv_w, conv_b, w_down, l)
        if l + 1 < depth:
            xs, xw, ssq, w_out_b = _downproj(g, w_down_b, x1, norm1_w[l + 1], w_out, l + 1)
        else:
            out = _downproj_final(g, w_down_b, x1, final_norm_w)
    return out.reshape(B, S, D)
```

```python
import functools
import math

import jax
import jax.numpy as jnp
from jax import lax
from jax.experimental import pallas as pl
from jax.experimental.pallas import tpu as pltpu

F32 = jnp.float32
BF16 = jnp.bfloat16

LANES = 128
SUBLANES = 8
BF16_SUBLANES = 16
CHUNK = 128
N_RET = 6
N_ATT = 6
N_GMLP = 4
DILATIONS = (1, 4, 16)
SPAN = CHUNK * DILATIONS[-1]
ROPE_BASE = 10000.0
EPS = 1e-6
NEG_INF = -1e30
VMEM_LIMIT = 56 * 1024 * 1024

RQ, RK, RV, RG = 0, N_RET, 2 * N_RET, 3 * N_RET
AQ = 4 * N_RET
AK = AQ + N_ATT
AV = AK + N_ATT
GU = AV + N_ATT
GV = GU + N_GMLP
N_SLABS = GV + N_GMLP


def _params(*sem):
    return pltpu.CompilerParams(dimension_semantics=sem, vmem_limit_bytes=VMEM_LIMIT)


def _sidecast(w, l, steps, step_of):
    _, rows, cols = w.shape
    band = rows // steps
    assert band * steps == rows and band % BF16_SUBLANES == 0
    return (pl.BlockSpec((None, band, cols), lambda *g: (l, step_of(*g), 0)),
            pl.BlockSpec((band, cols), lambda *g: (step_of(*g), 0)),
            jax.ShapeDtypeStruct((rows, cols), BF16))


def _rms(x, w):
    return x * lax.rsqrt(jnp.mean(x * x, axis=-1, keepdims=True) + EPS) * w


def _silu(x):
    hx = 0.5 * x
    return hx + hx * jnp.tanh(hx)


def _prescale_kernel(x_ref, w_ref, wc_ref, xw_ref, ssq_ref, wcb_ref):
    x = x_ref[...]
    xw_ref[...] = (x * w_ref[...]).astype(BF16)
    ssq_ref[...] = jnp.sum(x * x, axis=-1, keepdims=True)
    wcb_ref[...] = wc_ref[...].astype(BF16)


def _prescale(x, w, wc, lc, tm=512):
    S, D = x.shape
    wc_in, wc_out, wc_shape = _sidecast(wc, lc, S // tm, lambda i: i)
    return pl.pallas_call(
        _prescale_kernel,
        grid=(S // tm,),
        in_specs=[pl.BlockSpec((tm, D), lambda i: (i, 0)), pl.BlockSpec((1, D), lambda i: (0, 0)), wc_in],
        out_specs=[pl.BlockSpec((tm, D), lambda i: (i, 0)), pl.BlockSpec((None, tm, 1), lambda i: (0, i, 0)), wc_out],
        out_shape=[jax.ShapeDtypeStruct((S, D), BF16), jax.ShapeDtypeStruct((1, S, 1), F32), wc_shape],
        compiler_params=_params("parallel"),
        name="prescale",
    )(x, w.reshape(1, D), wc)


def _inproj_kernel(a_ref, ssq_ref, w_ref, o_ref, wb):
    @pl.when(pl.program_id(1) == 0)
    def _():
        wb[...] = w_ref[...].astype(BF16)

    D = a_ref.shape[1]
    r = lax.rsqrt(sum(ssq_ref[p] for p in range(ssq_ref.shape[0])) / D + EPS)
    acc = jnp.dot(a_ref[...], wb[...], preferred_element_type=F32)
    for j in range(o_ref.shape[0]):
        o_ref[j] = acc[:, j * LANES:(j + 1) * LANES] * r


def _inproj(xw, ssq, w, l, tm=1024, tn=1280):
    S, D = xw.shape
    N = w.shape[2]
    parts = ssq.shape[0]
    return pl.pallas_call(
        _inproj_kernel,
        grid=(N // tn, S // tm),
        in_specs=[pl.BlockSpec((tm, D), lambda j, i: (i, 0)), pl.BlockSpec((parts, tm, 1), lambda j, i: (0, i, 0)),
                  pl.BlockSpec((None, D, tn), lambda j, i: (l, 0, j))],
        out_specs=pl.BlockSpec((tn // LANES, tm, LANES), lambda j, i: (j, i, 0)),
        out_shape=jax.ShapeDtypeStruct((N // LANES, S, LANES), F32),
        scratch_shapes=[pltpu.VMEM((D, tn), BF16)],
        compiler_params=_params("parallel", "arbitrary"),
        name="inproj",
    )(xw, ssq, w)


def _retention_kernel(q_ref, k_ref, v_ref, g_ref, cos_ref, sin_ref, dec_ref, xi_ref, zeta_ref, cd_ref,
                      nw_ref, o_ref, state_ref, lbuf, ubuf, sbuf):
    @pl.when(pl.program_id(1) == 0)
    def _():
        state_ref[...] = jnp.zeros_like(state_ref)

    scale = LANES ** -0.5
    nh, tr = q_ref.shape[0], q_ref.shape[1]
    nc = tr // CHUNK
    for c in range(nc):
        rows = pl.ds(c * CHUNK, CHUNK)
        cos, sin = cos_ref[rows, :], sin_ref[rows, :]
        for h in range(nh):
            u = c * nh + h
            q, k = q_ref[h, rows, :], k_ref[h, rows, :]
            qr = q * cos + pltpu.roll(q, LANES // 2, 1) * sin
            kr = (k * cos + pltpu.roll(k, LANES // 2, 1) * sin) * scale
            vb = v_ref[h, rows, :].astype(BF16)
            s = lax.dot_general(qr.astype(BF16), kr.astype(BF16), (((1,), (1,)), ((), ())),
                                preferred_element_type=F32) * dec_ref[h]
            lbuf[u, :, 0:CHUNK] = s.astype(BF16)
            lbuf[u, :, CHUNK:CHUNK + LANES] = (qr * xi_ref[h]).astype(BF16)
            ubuf[u] = lax.dot_general((kr * zeta_ref[h]).astype(BF16), vb, (((0,), (0,)), ((), ())),
                                      preferred_element_type=F32)
    for h in range(nh):
        state = state_ref[h]
        for c in range(nc):
            sbuf[c * nh + h] = state.astype(BF16)
            state = cd_ref[h] * state + ubuf[c * nh + h]
        state_ref[h] = state
    for c in range(nc):
        rows = pl.ds(c * CHUNK, CHUNK)
        for h in range(nh):
            u = c * nh + h
            rhs = jnp.concatenate([v_ref[h, rows, :].astype(BF16), sbuf[u]], axis=0)
            o = jnp.dot(lbuf[u], rhs, preferred_element_type=F32)
            o_ref[h, rows, :] = (_silu(g_ref[h, rows, :]) * _rms(o, nw_ref[h])).astype(o_ref.dtype)


def _retention_tables(S):
    half = LANES // 2
    inv_freq = ROPE_BASE ** (-jnp.arange(half, dtype=F32) / half)
    ang = jnp.arange(S, dtype=jnp.int32).astype(F32)[:, None] * inv_freq[None, :]
    cos, sin = jnp.cos(ang), jnp.sin(ang)
    cosf = jnp.concatenate([cos, cos], axis=-1)
    sinf = jnp.concatenate([-sin, sin], axis=-1)
    log_g = jnp.log1p(-jnp.exp2(-5.0 - jnp.arange(N_RET, dtype=F32)))
    idx = jnp.arange(CHUNK, dtype=F32)
    diff = idx[:, None] - idx[None, :]
    dec = jnp.where(diff[None] >= 0, jnp.exp(jnp.maximum(diff, 0.0)[None] * log_g[:, None, None]), 0.0)
    xi = jnp.exp((idx[None, :] + 1.0) * log_g[:, None])
    zeta = jnp.exp((CHUNK - 1.0 - idx)[None, :] * log_g[:, None])
    cd = jnp.exp(CHUNK * log_g)
    bcast = lambda t: jnp.broadcast_to(t[:, :, None], (N_RET, CHUNK, LANES))
    return cosf, sinf, dec, bcast(xi), bcast(zeta), jnp.broadcast_to(cd[:, None, None], (N_RET, 1, LANES))


def _retention(proj, tables, norm_w, tr=1024, nh=N_RET):
    S = proj.shape[1]
    cosf, sinf, dec, xi, zeta, cd = tables
    assert all(off % nh == 0 for off in (RQ, RK, RV, RG)) and N_RET % nh == 0
    slab = lambda off: pl.BlockSpec((nh, tr, LANES), lambda h, i: (off // nh + h, i, 0))
    tab = pl.BlockSpec((tr, LANES), lambda h, i: (i, 0))
    per_head = lambda r: pl.BlockSpec((nh, r, LANES), lambda h, i: (h, 0, 0))
    return pl.pallas_call(
        _retention_kernel,
        grid=(N_RET // nh, S // tr),
        in_specs=[slab(RQ), slab(RK), slab(RV), slab(RG), tab, tab,
                  per_head(CHUNK), per_head(CHUNK), per_head(CHUNK), per_head(1), per_head(1)],
        out_specs=pl.BlockSpec((nh, tr, LANES), lambda h, i: (h, i, 0)),
        out_shape=jax.ShapeDtypeStruct((N_RET, S, LANES), BF16),
        scratch_shapes=[pltpu.VMEM((nh, LANES, LANES), F32),
                        pltpu.VMEM((nh * tr // CHUNK, CHUNK, CHUNK + LANES), BF16),
                        pltpu.VMEM((nh * tr // CHUNK, LANES, LANES), F32),
                        pltpu.VMEM((nh * tr // CHUNK, LANES, LANES), BF16)],
        compiler_params=_params("parallel", "arbitrary"),
        name="retention",
    )(proj, proj, proj, proj, cosf, sinf, dec, xi, zeta, cd, norm_w.reshape(N_RET, 1, LANES))


D2 = DILATIONS[1]
TROWS = SPAN // D2
UNITS = 3 * (SPAN // CHUNK)
assert DILATIONS == (1, D2, D2 * D2) and SPAN == CHUNK * D2 * D2


def _dilated_kernel(q_ref, k_ref, v_ref, o_ref, tq, tk, tv, kt, vt, sbuf, o1, d1, m1, o2, d2, m2, o3, d3, m3):
    n = pl.program_id(1)
    slot = n % 2
    pslot = 1 - slot
    nblk = SPAN // CHUNK
    scale2 = LANES ** -0.5 * math.log2(math.e)
    blk = lambda i: pl.ds(i * CHUNK, CHUNK)
    strided = lambda start: pl.ds(start, CHUNK, stride=D2)

    @pl.when(n == 0)
    def _():
        tk[1] = jnp.zeros(tk.shape[1:], F32)
        tv[1] = jnp.zeros(tv.shape[1:], F32)
        kt[...] = jnp.zeros(kt.shape, F32)
        vt[...] = jnp.zeros(vt.shape, F32)

    for r in range(D2):
        tq[r] = q_ref[pl.ds(r, TROWS, stride=D2), :]
        tk[slot, r] = k_ref[pl.ds(r, TROWS, stride=D2), :]
        tv[slot, r] = v_ref[pl.ds(r, TROWS, stride=D2), :]

    qi = lax.broadcasted_iota(jnp.int32, (CHUNK, CHUNK), 0)
    ci = lax.broadcasted_iota(jnp.int32, (CHUNK, CHUNK), 1)
    own_ok = ci <= qi
    prev_ok = ci >= qi
    prev_ok_span = ci >= qi + jnp.where(n > 0, 0, CHUNK)

    def units(it):
        ss, r = it // D2, it % D2
        narrow_prev = (lambda kv: kv[2][...]) if it == 0 else (lambda kv: kv[0][blk(it - 1), :])
        mid_prev = ((lambda kv: kv[1][slot, r, blk(ss - 1), :]) if ss > 0 else
                    (lambda kv: kv[1][pslot, r, blk(D2 - 1), :]))
        return (
            (lambda: q_ref[blk(it), :], narrow_prev, lambda kv: kv[0][blk(it), :],
             prev_ok if it > 0 else prev_ok_span, (blk(it), slice(None))),
            (lambda: tq[r, blk(ss), :], mid_prev, lambda kv: kv[1][slot, r, blk(ss), :],
             prev_ok if ss > 0 else prev_ok_span, (r, blk(ss), slice(None))),
            (lambda: tq[r, strided(ss), :], lambda kv: kv[1][pslot, r, strided(ss), :],
             lambda kv: kv[1][slot, r, strided(ss), :], prev_ok_span, (r, strided(ss), slice(None))),
        )

    keys = (k_ref, tk, kt)
    vals = (v_ref, tv, vt)

    for it in range(nblk):
        for p, (load_q, load_prev, load_own, mask_prev, rows), m_s in zip(range(3), units(it), (m1, m2, m3)):
            k = jnp.concatenate([load_prev(keys), load_own(keys)], axis=0).astype(BF16)
            s = lax.dot_general(load_q().astype(BF16), k, (((1,), (1,)), ((), ())),
                                preferred_element_type=F32)
            sp = jnp.where(mask_prev, s[:, :CHUNK], NEG_INF)
            so = jnp.where(own_ok, s[:, CHUNK:], NEG_INF)
            sbuf[3 * it + p, :, 0:CHUNK] = sp
            sbuf[3 * it + p, :, CHUNK:2 * CHUNK] = so
            m_s[rows] = jnp.broadcast_to(jnp.max(jnp.maximum(sp, so), axis=-1, keepdims=True), (CHUNK, LANES))

    ones = jnp.ones((2 * CHUNK, LANES), BF16)
    for it in range(nblk):
        for p, (_, load_prev, load_own, _, rows), o_s, d_s, m_s in zip(
                range(3), units(it), (o1, o2, o3), (d1, d2, d3), (m1, m2, m3)):
            m = m_s[rows]
            e = jnp.concatenate([jnp.exp2((sbuf[3 * it + p, :, 0:CHUNK] - m) * scale2),
                                 jnp.exp2((sbuf[3 * it + p, :, CHUNK:2 * CHUNK] - m) * scale2)], axis=1)
            v = jnp.concatenate([load_prev(vals), load_own(vals)], axis=0).astype(BF16)
            oa = jnp.dot(e.astype(BF16), jnp.concatenate([v, ones], axis=1), preferred_element_type=F32)
            o_s[rows] = oa[:, :LANES]
            d_s[rows] = oa[:, LANES:]

    kt[...] = k_ref[blk(nblk - 1), :]
    vt[...] = v_ref[blk(nblk - 1), :]

    def phase_c(j, carry):
        for r in range(D2):
            nat = (pl.ds(j * CHUNK * D2 + r, CHUNK, stride=D2), slice(None))
            trw = (r, pl.ds(j * CHUNK, CHUNK), slice(None))
            ma, mb, mc = m1[nat], m2[trw], m3[trw]
            mx = jnp.maximum(jnp.maximum(ma, mb), mc)
            wa, wb, wc = (jnp.exp2((t - mx) * scale2) for t in (ma, mb, mc))
            num = wa * o1[nat] + wb * o2[trw] + wc * o3[trw]
            den = wa * d1[nat] + wb * d2[trw] + wc * d3[trw]
            o_ref[nat] = num / den
        return carry

    lax.fori_loop(0, TROWS // CHUNK, phase_c, 0)


def _dilated_attention(proj):
    S = proj.shape[1]
    slab = lambda off: pl.BlockSpec((None, SPAN, LANES), lambda h, n: (off + h, n, 0))
    nat = pltpu.VMEM((SPAN, LANES), F32)
    res = pltpu.VMEM((D2, TROWS, LANES), F32)
    ring = pltpu.VMEM((2, D2, TROWS, LANES), F32)
    tail = pltpu.VMEM((CHUNK, LANES), F32)
    return pl.pallas_call(
        _dilated_kernel,
        grid=(N_ATT, S // SPAN),
        in_specs=[slab(AQ), slab(AK), slab(AV)],
        out_specs=pl.BlockSpec((None, SPAN, LANES), lambda h, n: (h, n, 0)),
        out_shape=jax.ShapeDtypeStruct((N_ATT, S, LANES), F32),
        scratch_shapes=[res, ring, ring, tail, tail, pltpu.VMEM((UNITS, CHUNK, 2 * CHUNK), F32),
                        nat, nat, nat, res, res, res, res, res, res],
        compiler_params=_params("parallel", "arbitrary"),
        name="dilated_attention",
    )(proj, proj, proj)


def _gmlp_kernel(*refs):
    G = N_GMLP
    u_refs, v_refs = refs[:G], refs[G:2 * G]
    lnw_ref, ws_ref, bs_ref, ow_ref, o_ref = refs[2 * G:]
    tm = o_ref.shape[1]
    width = G * LANES
    gv = [jax.nn.gelu(r[...]) for r in v_refs]
    mu = sum(jnp.sum(t, axis=-1, keepdims=True) for t in gv) / width
    xc = [t - mu for t in gv]
    inv = lax.rsqrt(sum(jnp.sum(t * t, axis=-1, keepdims=True) for t in xc) / width + EPS)
    qi = lax.broadcasted_iota(jnp.int32, (CHUNK, CHUNK), 0)
    ki = lax.broadcasted_iota(jnp.int32, (CHUNK, CHUNK), 1)
    outs = []
    for g in range(G):
        vn = (xc[g] * inv * lnw_ref[g]).astype(BF16)
        wm = jnp.where(ki <= qi, ws_ref[g], 0.0).astype(BF16)
        sp = [jnp.dot(wm, vn[c * CHUNK:(c + 1) * CHUNK, :], preferred_element_type=F32) + bs_ref[g]
              for c in range(tm // CHUNK)]
        outs.append(jax.nn.gelu(u_refs[g][...]) * jnp.concatenate(sp, axis=0))
    inv_o = lax.rsqrt(sum(jnp.sum(t * t, axis=-1, keepdims=True) for t in outs) / width + EPS)
    for g in range(G):
        o_ref[g] = (outs[g] * inv_o * ow_ref[g]).astype(o_ref.dtype)


def _gmlp(proj, ln_w, ws, bs, out_w, tm=1024):
    S = proj.shape[1]
    G = N_GMLP
    slab = lambda s: pl.BlockSpec((None, tm, LANES), lambda i: (s, i, 0))
    small = lambda shape: pl.BlockSpec(shape, lambda i: (0,) * len(shape))
    return pl.pallas_call(
        _gmlp_kernel,
        grid=(S // tm,),
        in_specs=[slab(GU + g) for g in range(G)] + [slab(GV + g) for g in range(G)]
        + [small((G, 1, LANES)), small((G, CHUNK, CHUNK)), small((G, CHUNK, 1)), small((G, 1, LANES))],
        out_specs=pl.BlockSpec((G, tm, LANES), lambda i: (0, i, 0)),
        out_shape=jax.ShapeDtypeStruct((G, S, LANES), BF16),
        compiler_params=_params("parallel"),
        name="gmlp",
    )(*([proj] * (2 * G)), ln_w.reshape(G, 1, LANES), ws, bs.reshape(G, CHUNK, 1), out_w.reshape(G, 1, LANES))


def _outproj_kernel(ret_ref, att_ref, gm_ref, x_ref, attw_ref, wb, n2w_ref, x1_ref, h2_ref, a_scr, *, nc):
    col = lambda j: slice(j * LANES, (j + 1) * LANES)
    for j in range(N_RET):
        a_scr[:, col(j)] = ret_ref[j].astype(BF16)
    att = [att_ref[j] for j in range(N_ATT)]
    inv = lax.rsqrt(sum(jnp.sum(t * t, axis=-1, keepdims=True) for t in att) / (N_ATT * LANES) + EPS)
    for j in range(N_ATT):
        a_scr[:, col(N_RET + j)] = (att[j] * inv * attw_ref[j]).astype(BF16)
    for j in range(N_GMLP):
        a_scr[:, col(N_RET + N_ATT + j)] = gm_ref[j].astype(BF16)
    D = x_ref.shape[1]
    ssq = 0.0
    for c in range(D // nc):
        cs = slice(c * nc, (c + 1) * nc)
        x1 = x_ref[:, cs] + jnp.dot(a_scr[...], wb[:, cs], preferred_element_type=F32)
        x1_ref[:, cs] = x1
        ssq = ssq + jnp.sum(x1 * x1, axis=-1, keepdims=True)
    h2_ref[...] = (x1_ref[...] * lax.rsqrt(ssq / D + EPS) * n2w_ref[...]).astype(BF16)


def _outproj(ret, att, gm, x, att_w, w, n2w, tm=512, nc=512):
    S, D = x.shape
    K = w.shape[0]
    rows = lambda n: pl.BlockSpec((n, tm, LANES), lambda i: (0, i, 0))
    full = lambda: pl.BlockSpec((tm, D), lambda i: (i, 0))
    const = lambda shape: pl.BlockSpec(shape, lambda i: (0,) * len(shape))
    return pl.pallas_call(
        functools.partial(_outproj_kernel, nc=nc),
        grid=(S // tm,),
        in_specs=[rows(N_RET), rows(N_ATT), rows(N_GMLP), full(), const((N_ATT, 1, LANES)),
                  pl.BlockSpec((K, D), lambda i: (0, 0), pipeline_mode=pl.Buffered(1)), const((1, D))],
        out_specs=[full(), full()],
        out_shape=[jax.ShapeDtypeStruct((S, D), F32), jax.ShapeDtypeStruct((S, D), BF16)],
        scratch_shapes=[pltpu.VMEM((tm, K), BF16)],
        compiler_params=_params("parallel"),
        name="outproj",
    )(ret, att, gm, x, att_w.reshape(N_ATT, 1, LANES), w, n2w.reshape(1, D))


def _upproj_kernel(a_ref, wg_ref, wv_ref, cwg_ref, cwv_ref, cbg_ref, cbv_ref, wc_ref, o_ref, wcb_ref, wgb, wvb, hg, hv):
    wcb_ref[...] = wc_ref[...].astype(BF16)
    tm = a_ref.shape[0]
    nslab = hg.shape[0]
    head = SUBLANES

    @pl.when(pl.program_id(1) == 0)
    def _():
        wgb[...] = wg_ref[...].astype(BF16)
        wvb[...] = wv_ref[...].astype(BF16)
        hg[:, 0:head, :] = jnp.zeros((nslab, head, LANES), F32)
        hv[:, 0:head, :] = jnp.zeros((nslab, head, LANES), F32)

    a = a_ref[...]

    def conv(wb, cw_ref, cb_ref, hbuf):
        h = jnp.dot(a, wb[...], preferred_element_type=F32)
        outs = []
        for s in range(nslab):
            cs = slice(s * LANES, (s + 1) * LANES)
            hs = h[:, cs]
            hbuf[s, head:head + tm, :] = hs
            outs.append(cb_ref[:, cs] + cw_ref[0:1, cs] * hbuf[s, head - 2:head - 2 + tm, :]
                        + cw_ref[1:2, cs] * hbuf[s, head - 1:head - 1 + tm, :] + cw_ref[2:3, cs] * hs)
            hbuf[s, 0:head, :] = hbuf[s, tm:tm + head, :]
        return outs

    gate = conv(wgb, cwg_ref, cbg_ref, hg)
    val = conv(wvb, cwv_ref, cbv_ref, hv)
    for s in range(nslab):
        o_ref[:, s * LANES:(s + 1) * LANES] = (_silu(gate[s]) * val[s]).astype(o_ref.dtype)


def _upproj(h2, w_up, conv_w, conv_b, wc, l, tm=1024, tn=512):
    S, D = h2.shape
    F = w_up.shape[2] // 2
    nj, nm = F // tn, S // tm
    taps = conv_w.shape[1]
    hbuf = pltpu.VMEM((tn // LANES, tm + SUBLANES, LANES), F32)
    wc_in, wc_out, wc_shape = _sidecast(wc, l, nj * nm, lambda j, i: j * nm + i)
    return pl.pallas_call(
        _upproj_kernel,
        grid=(nj, nm),
        in_specs=[pl.BlockSpec((tm, D), lambda j, i: (i, 0)),
                  pl.BlockSpec((None, D, tn), lambda j, i: (l, 0, j)),
                  pl.BlockSpec((None, D, tn), lambda j, i: (l, 0, j + nj)),
                  pl.BlockSpec((None, taps, tn), lambda j, i: (l, 0, j)),
                  pl.BlockSpec((None, taps, tn), lambda j, i: (l, 0, j + nj)),
                  pl.BlockSpec((None, 1, tn), lambda j, i: (l, 0, j)),
                  pl.BlockSpec((None, 1, tn), lambda j, i: (l, 0, j + nj)),
                  wc_in],
        out_specs=[pl.BlockSpec((tm, tn), lambda j, i: (i, j)), wc_out],
        out_shape=[jax.ShapeDtypeStruct((S, F), BF16), wc_shape],
        scratch_shapes=[pltpu.VMEM((D, tn), BF16), pltpu.VMEM((D, tn), BF16), hbuf, hbuf],
        compiler_params=_params("parallel", "arbitrary"),
        name="upproj",
    )(h2, w_up, w_up, conv_w, conv_w, conv_b[:, None, :], conv_b[:, None, :], wc)


def _downproj_kernel(a_ref, w_ref, x1_ref, nw_ref, wc_ref, o_ref, xw_ref, ssq_ref, wcb_ref):
    x2 = x1_ref[...] + jnp.dot(a_ref[...], w_ref[...], preferred_element_type=F32)
    o_ref[...] = x2
    xw_ref[...] = (x2 * nw_ref[...]).astype(BF16)
    ssq_ref[...] = jnp.sum(x2 * x2, axis=-1, keepdims=True)
    wcb_ref[...] = wc_ref[...].astype(BF16)


def _downproj(g, w, x1, next_w, wc, lc, tm=512, tn=1024):
    S, F = g.shape
    D = w.shape[1]
    nm = S // tm
    tile = lambda: pl.BlockSpec((tm, tn), lambda j, i: (i, j))
    wc_in, wc_out, wc_shape = _sidecast(wc, lc, (D // tn) * nm, lambda j, i: j * nm + i)
    return pl.pallas_call(
        _downproj_kernel,
        grid=(D // tn, nm),
        in_specs=[pl.BlockSpec((tm, F), lambda j, i: (i, 0)), pl.BlockSpec((F, tn), lambda j, i: (0, j)), tile(),
                  pl.BlockSpec((1, tn), lambda j, i: (0, j)), wc_in],
        out_specs=[tile(), tile(), pl.BlockSpec((None, tm, 1), lambda j, i: (j, i, 0)), wc_out],
        out_shape=[jax.ShapeDtypeStruct((S, D), F32), jax.ShapeDtypeStruct((S, D), BF16),
                   jax.ShapeDtypeStruct((D // tn, S, 1), F32), wc_shape],
        compiler_params=_params("parallel", "parallel"),
        name="downproj",
    )(g, w, x1, next_w.reshape(1, D), wc)


def _downproj_final_kernel(a_ref, w_ref, x1_ref, nw_ref, o_ref, *, nc):
    D = o_ref.shape[1]
    a = a_ref[...]
    ssq = 0.0
    for c in range(D // nc):
        cs = slice(c * nc, (c + 1) * nc)
        x2 = x1_ref[:, cs] + jnp.dot(a, w_ref[:, cs], preferred_element_type=F32)
        o_ref[:, cs] = x2
        ssq = ssq + jnp.sum(x2 * x2, axis=-1, keepdims=True)
    o_ref[...] = o_ref[...] * lax.rsqrt(ssq / D + EPS) * nw_ref[...]


def _downproj_final(g, w, x1, final_w, tm=512, nc=512):
    S, F = g.shape
    D = w.shape[1]
    rows = lambda: pl.BlockSpec((tm, D), lambda i: (i, 0))
    return pl.pallas_call(
        functools.partial(_downproj_final_kernel, nc=nc),
        grid=(S // tm,),
        in_specs=[pl.BlockSpec((tm, F), lambda i: (i, 0)),
                  pl.BlockSpec((F, D), lambda i: (0, 0), pipeline_mode=pl.Buffered(1)),
                  rows(), pl.BlockSpec((1, D), lambda i: (0, 0))],
        out_specs=rows(),
        out_shape=jax.ShapeDtypeStruct((S, D), F32),
        compiler_params=_params("parallel"),
        name="downproj_final",
    )(g, w, x1, final_w.reshape(1, D))


def kernel(x, norm1_w, w_in, ret_norm_w, att_norm_w, gmlp_ln_w, gmlp_ws, gmlp_bs, gmlp_out_w, w_out, norm2_w, w_up, conv_w, conv_b, w_down, final_norm_w):
    B, S, D = x.shape
    depth = w_in.shape[0]
    assert B == 1 and S % SPAN == 0 and w_in.shape[2] == N_SLABS * LANES
    tables = _retention_tables(S)
    xs = x.reshape(S, D)
    xw, ssq, w_out_b = _prescale(xs, norm1_w[0], w_out, 0)
    for l in range(depth):
        proj = _inproj(xw, ssq, w_in, l)
        ret = _retention(proj, tables, ret_norm_w[l])
        att = _dilated_attention(proj)
        gm = _gmlp(proj, gmlp_ln_w[l], gmlp_ws[l], gmlp_bs[l], gmlp_out_w[l])
        x1, h2 = _outproj(ret, att, gm, xs, att_norm_w[l], w_out_b, norm2_w[l])
        g, w_down_b = _upproj(h2, w_up, conv_w, conv_b, w_down, l)
        if l + 1 < depth:
            xs, xw, ssq, w_out_b = _downproj(g, w_down_b, x1, norm1_w[l + 1], w_out, l + 1)
        else:
            out = _downproj_final(g, w_down_b, x1, final_norm_w)
    return out.reshape(B, S, D)
```

```python
import functools
import math

import jax
import jax.numpy as jnp
from jax import lax
from jax.experimental import pallas as pl
from jax.experimental.pallas import tpu as pltpu

F32 = jnp.float32
BF16 = jnp.bfloat16

LANES = 128
SUBLANES = 8
BF16_SUBLANES = 16
CHUNK = 128
N_RET = 6
N_ATT = 6
N_GMLP = 4
DILATIONS = (1, 4, 16)
SPAN = CHUNK * DILATIONS[-1]
ROPE_BASE = 10000.0
EPS = 1e-6
NEG_INF = -1e30
VMEM_LIMIT = 56 * 1024 * 1024

RQ, RK, RV, RG = 0, N_RET, 2 * N_RET, 3 * N_RET
AQ = 4 * N_RET
AK = AQ + N_ATT
AV = AK + N_ATT
GU = AV + N_ATT
GV = GU + N_GMLP
N_SLABS = GV + N_GMLP


def _params(*sem):
    return pltpu.CompilerParams(dimension_semantics=sem, vmem_limit_bytes=VMEM_LIMIT)


def _sidecast(w, l, steps, step_of):
    _, rows, cols = w.shape
    band = rows // steps
    assert band * steps == rows and band % BF16_SUBLANES == 0
    return (pl.BlockSpec((None, band, cols), lambda *g: (l, step_of(*g), 0)),
            pl.BlockSpec((band, cols), lambda *g: (step_of(*g), 0)),
            jax.ShapeDtypeStruct((rows, cols), BF16))


def _rms(x, w):
    return x * lax.rsqrt(jnp.mean(x * x, axis=-1, keepdims=True) + EPS) * w


def _silu(x):
    hx = 0.5 * x
    return hx + hx * jnp.tanh(hx)


def _prescale_kernel(x_ref, w_ref, wc_ref, xw_ref, ssq_ref, wcb_ref):
    x = x_ref[...]
    xw_ref[...] = (x * w_ref[...]).astype(BF16)
    ssq_ref[...] = jnp.sum(x * x, axis=-1, keepdims=True)
    wcb_ref[...] = wc_ref[...].astype(BF16)


def _prescale(x, w, wc, lc, tm=512):
    S, D = x.shape
    wc_in, wc_out, wc_shape = _sidecast(wc, lc, S // tm, lambda i: i)
    return pl.pallas_call(
        _prescale_kernel,
        grid=(S // tm,),
        in_specs=[pl.BlockSpec((tm, D), lambda i: (i, 0)), pl.BlockSpec((1, D), lambda i: (0, 0)), wc_in],
        out_specs=[pl.BlockSpec((tm, D), lambda i: (i, 0)), pl.BlockSpec((None, tm, 1), lambda i: (0, i, 0)), wc_out],
        out_shape=[jax.ShapeDtypeStruct((S, D), BF16), jax.ShapeDtypeStruct((1, S, 1), F32), wc_shape],
        compiler_params=_params("parallel"),
        name="prescale",
    )(x, w.reshape(1, D), wc)


def _inproj_kernel(a_ref, ssq_ref, w_ref, o_ref, wb):
    @pl.when(pl.program_id(1) == 0)
    def _():
        wb[...] = w_ref[...].astype(BF16)

    D = a_ref.shape[1]
    r = lax.rsqrt(sum(ssq_ref[p] for p in range(ssq_ref.shape[0])) / D + EPS)
    acc = jnp.dot(a_ref[...], wb[...], preferred_element_type=F32)
    for j in range(o_ref.shape[0]):
        o_ref[j] = acc[:, j * LANES:(j + 1) * LANES] * r


def _inproj(xw, ssq, w, l, tm=1024, tn=1280):
    S, D = xw.shape
    N = w.shape[2]
    parts = ssq.shape[0]
    return pl.pallas_call(
        _inproj_kernel,
        grid=(N // tn, S // tm),
        in_specs=[pl.BlockSpec((tm, D), lambda j, i: (i, 0)), pl.BlockSpec((parts, tm, 1), lambda j, i: (0, i, 0)),
                  pl.BlockSpec((None, D, tn), lambda j, i: (l, 0, j))],
        out_specs=pl.BlockSpec((tn // LANES, tm, LANES), lambda j, i: (j, i, 0)),
        out_shape=jax.ShapeDtypeStruct((N // LANES, S, LANES), F32),
        scratch_shapes=[pltpu.VMEM((D, tn), BF16)],
        compiler_params=_params("parallel", "arbitrary"),
        name="inproj",
    )(xw, ssq, w)


def _retention_kernel(q_ref, k_ref, v_ref, g_ref, ra_ref, rb_ref, dec_ref, xi_ref, zeta_ref, cd_ref,
                      nw_ref, o_ref, state_ref, lbuf, ubuf, sbuf):
    @pl.when(pl.program_id(1) == 0)
    def _():
        state_ref[...] = jnp.zeros_like(state_ref)

    scale = LANES ** -0.5
    nh, tr = q_ref.shape[0], q_ref.shape[1]
    nc = tr // CHUNK
    cb, sb = rb_ref[0], rb_ref[1]
    for c in range(nc):
        rows = pl.ds(c * CHUNK, CHUNK)
        ca, sa, cas, sas = (ra_ref[t, pl.ds(c, 1), :] for t in range(4))
        cos = ca * cb - sa * sb
        sin = sas * cb + cas * sb
        for h in range(nh):
            u = c * nh + h
            q, k = q_ref[h, rows, :], k_ref[h, rows, :]
            qr = q * cos + pltpu.roll(q, LANES // 2, 1) * sin
            kr = (k * cos + pltpu.roll(k, LANES // 2, 1) * sin) * scale
            vb = v_ref[h, rows, :].astype(BF16)
            s = lax.dot_general(qr.astype(BF16), kr.astype(BF16), (((1,), (1,)), ((), ())),
                                preferred_element_type=F32) * dec_ref[h]
            lbuf[u, :, 0:CHUNK] = s.astype(BF16)
            lbuf[u, :, CHUNK:CHUNK + LANES] = (qr * xi_ref[h]).astype(BF16)
            ubuf[u] = lax.dot_general((kr * zeta_ref[h]).astype(BF16), vb, (((0,), (0,)), ((), ())),
                                      preferred_element_type=F32)
    for h in range(nh):
        state = state_ref[h]
        for c in range(nc):
            sbuf[c * nh + h] = state.astype(BF16)
            state = cd_ref[h] * state + ubuf[c * nh + h]
        state_ref[h] = state
    for c in range(nc):
        rows = pl.ds(c * CHUNK, CHUNK)
        for h in range(nh):
            u = c * nh + h
            rhs = jnp.concatenate([v_ref[h, rows, :].astype(BF16), sbuf[u]], axis=0)
            o = jnp.dot(lbuf[u], rhs, preferred_element_type=F32)
            o_ref[h, rows, :] = (_silu(g_ref[h, rows, :]) * _rms(o, nw_ref[h])).astype(o_ref.dtype)


def _retention_tables(S):
    half = LANES // 2
    inv_freq = ROPE_BASE ** (-jnp.arange(half, dtype=F32) / half)
    inv_freq = jnp.concatenate([inv_freq, inv_freq])
    sign = jnp.concatenate([-jnp.ones(half, F32), jnp.ones(half, F32)])
    base = jnp.arange(0, S, CHUNK, dtype=jnp.int32).astype(F32)[:, None] * inv_freq[None, :]
    offs = jnp.arange(CHUNK, dtype=jnp.int32).astype(F32)[:, None] * inv_freq[None, :]
    rot_a = jnp.stack([jnp.cos(base), jnp.sin(base), sign * jnp.cos(base), sign * jnp.sin(base)])
    rot_b = jnp.stack([jnp.cos(offs), jnp.sin(offs)])
    log_g = jnp.log1p(-jnp.exp2(-5.0 - jnp.arange(N_RET, dtype=F32)))
    idx = jnp.arange(CHUNK, dtype=F32)
    diff = idx[:, None] - idx[None, :]
    dec = jnp.where(diff[None] >= 0, jnp.exp(jnp.maximum(diff, 0.0)[None] * log_g[:, None, None]), 0.0)
    xi = jnp.exp((idx[None, :] + 1.0) * log_g[:, None])
    zeta = jnp.exp((CHUNK - 1.0 - idx)[None, :] * log_g[:, None])
    cd = jnp.exp(CHUNK * log_g)
    bcast = lambda t: jnp.broadcast_to(t[:, :, None], (N_RET, CHUNK, LANES))
    return rot_a, rot_b, dec, bcast(xi), bcast(zeta), jnp.broadcast_to(cd[:, None, None], (N_RET, 1, LANES))


def _retention(proj, tables, norm_w, tr=1024, nh=N_RET):
    S = proj.shape[1]
    rot_a, rot_b, dec, xi, zeta, cd = tables
    assert all(off % nh == 0 for off in (RQ, RK, RV, RG)) and N_RET % nh == 0
    slab = lambda off: pl.BlockSpec((nh, tr, LANES), lambda h, i: (off // nh + h, i, 0))
    per_head = lambda r: pl.BlockSpec((nh, r, LANES), lambda h, i: (h, 0, 0))
    return pl.pallas_call(
        _retention_kernel,
        grid=(N_RET // nh, S // tr),
        in_specs=[slab(RQ), slab(RK), slab(RV), slab(RG),
                  pl.BlockSpec((4, tr // CHUNK, LANES), lambda h, i: (0, i, 0)),
                  pl.BlockSpec((2, CHUNK, LANES), lambda h, i: (0, 0, 0)),
                  per_head(CHUNK), per_head(CHUNK), per_head(CHUNK), per_head(1), per_head(1)],
        out_specs=pl.BlockSpec((nh, tr, LANES), lambda h, i: (h, i, 0)),
        out_shape=jax.ShapeDtypeStruct((N_RET, S, LANES), BF16),
        scratch_shapes=[pltpu.VMEM((nh, LANES, LANES), F32),
                        pltpu.VMEM((nh * tr // CHUNK, CHUNK, CHUNK + LANES), BF16),
                        pltpu.VMEM((nh * tr // CHUNK, LANES, LANES), F32),
                        pltpu.VMEM((nh * tr // CHUNK, LANES, LANES), BF16)],
        compiler_params=_params("parallel", "arbitrary"),
        name="retention",
    )(proj, proj, proj, proj, rot_a, rot_b, dec, xi, zeta, cd, norm_w.reshape(N_RET, 1, LANES))


D2 = DILATIONS[1]
TROWS = SPAN // D2
UNITS = 3 * (SPAN // CHUNK)
assert DILATIONS == (1, D2, D2 * D2) and SPAN == CHUNK * D2 * D2


def _dilated_kernel(q_ref, k_ref, v_ref, o_ref, tq, tk, tv, kt, vt, sbuf, o1, d1, m1, o2, d2, m2, o3, d3, m3):
    n = pl.program_id(1)
    slot = n % 2
    pslot = 1 - slot
    nblk = SPAN // CHUNK
    scale2 = LANES ** -0.5 * math.log2(math.e)
    blk = lambda i: pl.ds(i * CHUNK, CHUNK)
    strided = lambda start: pl.ds(start, CHUNK, stride=D2)

    @pl.when(n == 0)
    def _():
        tk[1] = jnp.zeros(tk.shape[1:], F32)
        tv[1] = jnp.zeros(tv.shape[1:], F32)
        kt[...] = jnp.zeros(kt.shape, F32)
        vt[...] = jnp.zeros(vt.shape, F32)

    for r in range(D2):
        tq[r] = q_ref[pl.ds(r, TROWS, stride=D2), :]
        tk[slot, r] = k_ref[pl.ds(r, TROWS, stride=D2), :]
        tv[slot, r] = v_ref[pl.ds(r, TROWS, stride=D2), :]

    qi = lax.broadcasted_iota(jnp.int32, (CHUNK, CHUNK), 0)
    ci = lax.broadcasted_iota(jnp.int32, (CHUNK, CHUNK), 1)
    own_ok = ci <= qi
    prev_ok = ci >= qi
    prev_ok_span = ci >= qi + jnp.where(n > 0, 0, CHUNK)

    def units(it):
        ss, r = it // D2, it % D2
        narrow_prev = (lambda kv: kv[2][...]) if it == 0 else (lambda kv: kv[0][blk(it - 1), :])
        mid_prev = ((lambda kv: kv[1][slot, r, blk(ss - 1), :]) if ss > 0 else
                    (lambda kv: kv[1][pslot, r, blk(D2 - 1), :]))
        return (
            (lambda: q_ref[blk(it), :], narrow_prev, lambda kv: kv[0][blk(it), :],
             prev_ok if it > 0 else prev_ok_span, (blk(it), slice(None))),
            (lambda: tq[r, blk(ss), :], mid_prev, lambda kv: kv[1][slot, r, blk(ss), :],
             prev_ok if ss > 0 else prev_ok_span, (r, blk(ss), slice(None))),
            (lambda: tq[r, strided(ss), :], lambda kv: kv[1][pslot, r, strided(ss), :],
             lambda kv: kv[1][slot, r, strided(ss), :], prev_ok_span, (r, strided(ss), slice(None))),
        )

    keys = (k_ref, tk, kt)
    vals = (v_ref, tv, vt)

    for it in range(nblk):
        for p, (load_q, load_prev, load_own, mask_prev, rows), m_s in zip(range(3), units(it), (m1, m2, m3)):
            k = jnp.concatenate([load_prev(keys), load_own(keys)], axis=0).astype(BF16)
            s = lax.dot_general(load_q().astype(BF16), k, (((1,), (1,)), ((), ())),
                                preferred_element_type=F32)
            sp = jnp.where(mask_prev, s[:, :CHUNK], NEG_INF)
            so = jnp.where(own_ok, s[:, CHUNK:], NEG_INF)
            sbuf[3 * it + p, :, 0:CHUNK] = sp
            sbuf[3 * it + p, :, CHUNK:2 * CHUNK] = so
            m_s[rows] = jnp.broadcast_to(jnp.max(jnp.maximum(sp, so), axis=-1, keepdims=True), (CHUNK, LANES))

    ones = jnp.ones((2 * CHUNK, LANES), BF16)
    for it in range(nblk):
        for p, (_, load_prev, load_own, _, rows), o_s, d_s, m_s in zip(
                range(3), units(it), (o1, o2, o3), (d1, d2, d3), (m1, m2, m3)):
            m = m_s[rows]
            e = jnp.concatenate([jnp.exp2((sbuf[3 * it + p, :, 0:CHUNK] - m) * scale2),
                                 jnp.exp2((sbuf[3 * it + p, :, CHUNK:2 * CHUNK] - m) * scale2)], axis=1)
            v = jnp.concatenate([load_prev(vals), load_own(vals)], axis=0).astype(BF16)
            oa = jnp.dot(e.astype(BF16), jnp.concatenate([v, ones], axis=1), preferred_element_type=F32)
            o_s[rows] = oa[:, :LANES]
            d_s[rows] = oa[:, LANES:]

    kt[...] = k_ref[blk(nblk - 1), :]
    vt[...] = v_ref[blk(nblk - 1), :]

    def phase_c(j, carry):
        for r in range(D2):
            nat = (pl.ds(j * CHUNK * D2 + r, CHUNK, stride=D2), slice(None))
            trw = (r, pl.ds(j * CHUNK, CHUNK), slice(None))
            ma, mb, mc = m1[nat], m2[trw], m3[trw]
            mx = jnp.maximum(jnp.maximum(ma, mb), mc)
            wa, wb, wc = (jnp.exp2((t - mx) * scale2) for t in (ma, mb, mc))
            num = wa * o1[nat] + wb * o2[trw] + wc * o3[trw]
            den = wa * d1[nat] + wb * d2[trw] + wc * d3[trw]
            o_ref[nat] = num / den
        return carry

    lax.fori_loop(0, TROWS // CHUNK, phase_c, 0)


def _dilated_attention(proj):
    S = proj.shape[1]
    slab = lambda off: pl.BlockSpec((None, SPAN, LANES), lambda h, n: (off + h, n, 0))
    nat = pltpu.VMEM((SPAN, LANES), F32)
    res = pltpu.VMEM((D2, TROWS, LANES), F32)
    ring = pltpu.VMEM((2, D2, TROWS, LANES), F32)
    tail = pltpu.VMEM((CHUNK, LANES), F32)
    return pl.pallas_call(
        _dilated_kernel,
        grid=(N_ATT, S // SPAN),
        in_specs=[slab(AQ), slab(AK), slab(AV)],
        out_specs=pl.BlockSpec((None, SPAN, LANES), lambda h, n: (h, n, 0)),
        out_shape=jax.ShapeDtypeStruct((N_ATT, S, LANES), F32),
        scratch_shapes=[res, ring, ring, tail, tail, pltpu.VMEM((UNITS, CHUNK, 2 * CHUNK), F32),
                        nat, nat, nat, res, res, res, res, res, res],
        compiler_params=_params("parallel", "arbitrary"),
        name="dilated_attention",
    )(proj, proj, proj)


def _gmlp_kernel(*refs):
    G = N_GMLP
    u_refs, v_refs = refs[:G], refs[G:2 * G]
    lnw_ref, ws_ref, bs_ref, ow_ref, o_ref = refs[2 * G:]
    tm = o_ref.shape[1]
    width = G * LANES
    gv = [jax.nn.gelu(r[...]) for r in v_refs]
    mu = sum(jnp.sum(t, axis=-1, keepdims=True) for t in gv) / width
    xc = [t - mu for t in gv]
    inv = lax.rsqrt(sum(jnp.sum(t * t, axis=-1, keepdims=True) for t in xc) / width + EPS)
    qi = lax.broadcasted_iota(jnp.int32, (CHUNK, CHUNK), 0)
    ki = lax.broadcasted_iota(jnp.int32, (CHUNK, CHUNK), 1)
    outs = []
    for g in range(G):
        vn = (xc[g] * inv * lnw_ref[g]).astype(BF16)
        wm = jnp.where(ki <= qi, ws_ref[g], 0.0).astype(BF16)
        sp = [jnp.dot(wm, vn[c * CHUNK:(c + 1) * CHUNK, :], preferred_element_type=F32) + bs_ref[g]
              for c in range(tm // CHUNK)]
        outs.append(jax.nn.gelu(u_refs[g][...]) * jnp.concatenate(sp, axis=0))
    inv_o = lax.rsqrt(sum(jnp.sum(t * t, axis=-1, keepdims=True) for t in outs) / width + EPS)
    for g in range(G):
        o_ref[g] = (outs[g] * inv_o * ow_ref[g]).astype(o_ref.dtype)


def _gmlp(proj, ln_w, ws, bs, out_w, tm=1024):
    S = proj.shape[1]
    G = N_GMLP
    slab = lambda s: pl.BlockSpec((None, tm, LANES), lambda i: (s, i, 0))
    small = lambda shape: pl.BlockSpec(shape, lambda i: (0,) * len(shape))
    return pl.pallas_call(
        _gmlp_kernel,
        grid=(S // tm,),
        in_specs=[slab(GU + g) for g in range(G)] + [slab(GV + g) for g in range(G)]
        + [small((G, 1, LANES)), small((G, CHUNK, CHUNK)), small((G, CHUNK, 1)), small((G, 1, LANES))],
        out_specs=pl.BlockSpec((G, tm, LANES), lambda i: (0, i, 0)),
        out_shape=jax.ShapeDtypeStruct((G, S, LANES), BF16),
        compiler_params=_params("parallel"),
        name="gmlp",
    )(*([proj] * (2 * G)), ln_w.reshape(G, 1, LANES), ws, bs.reshape(G, CHUNK, 1), out_w.reshape(G, 1, LANES))


def _outproj_kernel(ret_ref, att_ref, gm_ref, x_ref, attw_ref, wb, n2w_ref, x1_ref, h2_ref, a_scr, *, nc):
    col = lambda j: slice(j * LANES, (j + 1) * LANES)
    for j in range(N_RET):
        a_scr[:, col(j)] = ret_ref[j].astype(BF16)
    att = [att_ref[j] for j in range(N_ATT)]
    inv = lax.rsqrt(sum(jnp.sum(t * t, axis=-1, keepdims=True) for t in att) / (N_ATT * LANES) + EPS)
    for j in range(N_ATT):
        a_scr[:, col(N_RET + j)] = (att[j] * inv * attw_ref[j]).astype(BF16)
    for j in range(N_GMLP):
        a_scr[:, col(N_RET + N_ATT + j)] = gm_ref[j].astype(BF16)
    D = x_ref.shape[1]
    ssq = 0.0
    for c in range(D // nc):
        cs = slice(c * nc, (c + 1) * nc)
        x1 = x_ref[:, cs] + jnp.dot(a_scr[...], wb[:, cs], preferred_element_type=F32)
        x1_ref[:, cs] = x1
        ssq = ssq + jnp.sum(x1 * x1, axis=-1, keepdims=True)
    h2_ref[...] = (x1_ref[...] * lax.rsqrt(ssq / D + EPS) * n2w_ref[...]).astype(BF16)


def _outproj(ret, att, gm, x, att_w, w, n2w, tm=512, nc=512):
    S, D = x.shape
    K = w.shape[0]
    rows = lambda n: pl.BlockSpec((n, tm, LANES), lambda i: (0, i, 0))
    full = lambda: pl.BlockSpec((tm, D), lambda i: (i, 0))
    const = lambda shape: pl.BlockSpec(shape, lambda i: (0,) * len(shape))
    return pl.pallas_call(
        functools.partial(_outproj_kernel, nc=nc),
        grid=(S // tm,),
        in_specs=[rows(N_RET), rows(N_ATT), rows(N_GMLP), full(), const((N_ATT, 1, LANES)),
                  pl.BlockSpec((K, D), lambda i: (0, 0), pipeline_mode=pl.Buffered(1)), const((1, D))],
        out_specs=[full(), full()],
        out_shape=[jax.ShapeDtypeStruct((S, D), F32), jax.ShapeDtypeStruct((S, D), BF16)],
        scratch_shapes=[pltpu.VMEM((tm, K), BF16)],
        compiler_params=_params("parallel"),
        name="outproj",
    )(ret, att, gm, x, att_w.reshape(N_ATT, 1, LANES), w, n2w.reshape(1, D))


def _upproj_kernel(a_ref, wg_ref, wv_ref, cwg_ref, cwv_ref, cbg_ref, cbv_ref, wc_ref, o_ref, wcb_ref, wgb, wvb, hg, hv):
    wcb_ref[...] = wc_ref[...].astype(BF16)
    tm = a_ref.shape[0]
    nslab = hg.shape[0]
    head = SUBLANES

    @pl.when(pl.program_id(1) == 0)
    def _():
        wgb[...] = wg_ref[...].astype(BF16)
        wvb[...] = wv_ref[...].astype(BF16)
        hg[:, 0:head, :] = jnp.zeros((nslab, head, LANES), F32)
        hv[:, 0:head, :] = jnp.zeros((nslab, head, LANES), F32)

    a = a_ref[...]

    def conv(wb, cw_ref, cb_ref, hbuf):
        h = jnp.dot(a, wb[...], preferred_element_type=F32)
        outs = []
        for s in range(nslab):
            cs = slice(s * LANES, (s + 1) * LANES)
            hs = h[:, cs]
            hbuf[s, head:head + tm, :] = hs
            outs.append(cb_ref[:, cs] + cw_ref[0:1, cs] * hbuf[s, head - 2:head - 2 + tm, :]
                        + cw_ref[1:2, cs] * hbuf[s, head - 1:head - 1 + tm, :] + cw_ref[2:3, cs] * hs)
            hbuf[s, 0:head, :] = hbuf[s, tm:tm + head, :]
        return outs

    gate = conv(wgb, cwg_ref, cbg_ref, hg)
    val = conv(wvb, cwv_ref, cbv_ref, hv)
    for s in range(nslab):
        o_ref[:, s * LANES:(s + 1) * LANES] = (_silu(gate[s]) * val[s]).astype(o_ref.dtype)


def _upproj(h2, w_up, conv_w, conv_b, wc, l, tm=1024, tn=512):
    S, D = h2.shape
    F = w_up.shape[2] // 2
    nj, nm = F // tn, S // tm
    taps = conv_w.shape[1]
    hbuf = pltpu.VMEM((tn // LANES, tm + SUBLANES, LANES), F32)
    wc_in, wc_out, wc_shape = _sidecast(wc, l, nj * nm, lambda j, i: j * nm + i)
    return pl.pallas_call(
        _upproj_kernel,
        grid=(nj, nm),
        in_specs=[pl.BlockSpec((tm, D), lambda j, i: (i, 0)),
                  pl.BlockSpec((None, D, tn), lambda j, i: (l, 0, j)),
                  pl.BlockSpec((None, D, tn), lambda j, i: (l, 0, j + nj)),
                  pl.BlockSpec((None, taps, tn), lambda j, i: (l, 0, j)),
                  pl.BlockSpec((None, taps, tn), lambda j, i: (l, 0, j + nj)),
                  pl.BlockSpec((None, 1, tn), lambda j, i: (l, 0, j)),
                  pl.BlockSpec((None, 1, tn), lambda j, i: (l, 0, j + nj)),
                  wc_in],
        out_specs=[pl.BlockSpec((tm, tn), lambda j, i: (i, j)), wc_out],
        out_shape=[jax.ShapeDtypeStruct((S, F), BF16), wc_shape],
        scratch_shapes=[pltpu.VMEM((D, tn), BF16), pltpu.VMEM((D, tn), BF16), hbuf, hbuf],
        compiler_params=_params("parallel", "arbitrary"),
        name="upproj",
    )(h2, w_up, w_up, conv_w, conv_w, conv_b[:, None, :], conv_b[:, None, :], wc)


def _downproj_kernel(a_ref, w_ref, x1_ref, nw_ref, wc_ref, o_ref, xw_ref, ssq_ref, wcb_ref):
    x2 = x1_ref[...] + jnp.dot(a_ref[...], w_ref[...], preferred_element_type=F32)
    o_ref[...] = x2
    xw_ref[...] = (x2 * nw_ref[...]).astype(BF16)
    ssq_ref[...] = jnp.sum(x2 * x2, axis=-1, keepdims=True)
    wcb_ref[...] = wc_ref[...].astype(BF16)


def _downproj(g, w, x1, next_w, wc, lc, tm=512, tn=1024):
    S, F = g.shape
    D = w.shape[1]
    nm = S // tm
    tile = lambda: pl.BlockSpec((tm, tn), lambda j, i: (i, j))
    wc_in, wc_out, wc_shape = _sidecast(wc, lc, (D // tn) * nm, lambda j, i: j * nm + i)
    return pl.pallas_call(
        _downproj_kernel,
        grid=(D // tn, nm),
        in_specs=[pl.BlockSpec((tm, F), lambda j, i: (i, 0)), pl.BlockSpec((F, tn), lambda j, i: (0, j)), tile(),
                  pl.BlockSpec((1, tn), lambda j, i: (0, j)), wc_in],
        out_specs=[tile(), tile(), pl.BlockSpec((None, tm, 1), lambda j, i: (j, i, 0)), wc_out],
        out_shape=[jax.ShapeDtypeStruct((S, D), F32), jax.ShapeDtypeStruct((S, D), BF16),
                   jax.ShapeDtypeStruct((D // tn, S, 1), F32), wc_shape],
        compiler_params=_params("parallel", "parallel"),
        name="downproj",
    )(g, w, x1, next_w.reshape(1, D), wc)


def _downproj_final_kernel(a_ref, w_ref, x1_ref, nw_ref, o_ref, *, nc):
    D = o_ref.shape[1]
    a = a_ref[...]
    ssq = 0.0
    for c in range(D // nc):
        cs = slice(c * nc, (c + 1) * nc)
        x2 = x1_ref[:, cs] + jnp.dot(a, w_ref[:, cs], preferred_element_type=F32)
        o_ref[:, cs] = x2
        ssq = ssq + jnp.sum(x2 * x2, axis=-1, keepdims=True)
    o_ref[...] = o_ref[...] * lax.rsqrt(ssq / D + EPS) * nw_ref[...]


def _downproj_final(g, w, x1, final_w, tm=512, nc=512):
    S, F = g.shape
    D = w.shape[1]
    rows = lambda: pl.BlockSpec((tm, D), lambda i: (i, 0))
    return pl.pallas_call(
        functools.partial(_downproj_final_kernel, nc=nc),
        grid=(S // tm,),
        in_specs=[pl.BlockSpec((tm, F), lambda i: (i, 0)),
                  pl.BlockSpec((F, D), lambda i: (0, 0), pipeline_mode=pl.Buffered(1)),
                  rows(), pl.BlockSpec((1, D), lambda i: (0, 0))],
        out_specs=rows(),
        out_shape=jax.ShapeDtypeStruct((S, D), F32),
        compiler_params=_params("parallel"),
        name="downproj_final",
    )(g, w, x1, final_w.reshape(1, D))


def kernel(x, norm1_w, w_in, ret_norm_w, att_norm_w, gmlp_ln_w, gmlp_ws, gmlp_bs, gmlp_out_w, w_out, norm2_w, w_up, conv_w, conv_b, w_down, final_norm_w):
    B, S, D = x.shape
    depth = w_in.shape[0]
    assert B == 1 and S % SPAN == 0 and w_in.shape[2] == N_SLABS * LANES
    tables = _retention_tables(S)
    xs = x.reshape(S, D)
    xw, ssq, w_out_b = _prescale(xs, norm1_w[0], w_out, 0)
    for l in range(depth):
        proj = _inproj(xw, ssq, w_in, l)
        ret = _retention(proj, tables, ret_norm_w[l])
        att = _dilated_attention(proj)
        gm = _gmlp(proj, gmlp_ln_w[l], gmlp_ws[l], gmlp_bs[l], gmlp_out_w[l])
        x1, h2 = _outproj(ret, att, gm, xs, att_norm_w[l], w_out_b, norm2_w[l])
        g, w_down_b = _upproj(h2, w_up, conv_w, conv_b, w_down, l)
        if l + 1 < depth:
            xs, xw, ssq, w_out_b = _downproj(g, w_down_b, x1, norm1_w[l + 1], w_out, l + 1)
        else:
            out = _downproj_final(g, w_down_b, x1, final_norm_w)
    return out.reshape(B, S, D)
```

```python
import functools
import math

import jax
import jax.numpy as jnp
from jax import lax
from jax.experimental import pallas as pl
from jax.experimental.pallas import tpu as pltpu

F32 = jnp.float32
BF16 = jnp.bfloat16

LANES = 128
SUBLANES = 8
BF16_SUBLANES = 16
CHUNK = 128
N_RET = 6
N_ATT = 6
N_GMLP = 4
DILATIONS = (1, 4, 16)
SPAN = CHUNK * DILATIONS[-1]
ROPE_BASE = 10000.0
EPS = 1e-6
NEG_INF = -1e30
VMEM_LIMIT = 56 * 1024 * 1024

RQ, RK, RV, RG = 0, N_RET, 2 * N_RET, 3 * N_RET
AQ = 4 * N_RET
AK = AQ + N_ATT
AV = AK + N_ATT
GU = AV + N_ATT
GV = GU + N_GMLP
N_SLABS = GV + N_GMLP


def _params(*sem):
    return pltpu.CompilerParams(dimension_semantics=sem, vmem_limit_bytes=VMEM_LIMIT)


def _sidecast(w, l, steps, step_of):
    _, rows, cols = w.shape
    band = rows // steps
    assert band * steps == rows and band % BF16_SUBLANES == 0
    return (pl.BlockSpec((None, band, cols), lambda *g: (l, step_of(*g), 0)),
            pl.BlockSpec((band, cols), lambda *g: (step_of(*g), 0)),
            jax.ShapeDtypeStruct((rows, cols), BF16))


def _rms(x, w):
    return x * lax.rsqrt(jnp.mean(x * x, axis=-1, keepdims=True) + EPS) * w


def _silu(x):
    hx = 0.5 * x
    return hx + hx * jnp.tanh(hx)


def _inproj_kernel(a_ref, ssq_ref, w_ref, o_ref, wb):
    @pl.when(pl.program_id(1) == 0)
    def _():
        wb[...] = w_ref[...].astype(BF16)

    D = a_ref.shape[1]
    r = lax.rsqrt(sum(ssq_ref[p] for p in range(ssq_ref.shape[0])) / D + EPS)
    acc = jnp.dot(a_ref[...], wb[...], preferred_element_type=F32)
    for j in range(o_ref.shape[0]):
        o_ref[j] = acc[:, j * LANES:(j + 1) * LANES] * r


def _inproj(xw, ssq, w, l, tm=1024, tn=1280):
    S, D = xw.shape
    N = w.shape[2]
    parts = ssq.shape[0]
    return pl.pallas_call(
        _inproj_kernel,
        grid=(N // tn, S // tm),
        in_specs=[pl.BlockSpec((tm, D), lambda j, i: (i, 0)), pl.BlockSpec((parts, tm, 1), lambda j, i: (0, i, 0)),
                  pl.BlockSpec((None, D, tn), lambda j, i: (l, 0, j))],
        out_specs=pl.BlockSpec((tn // LANES, tm, LANES), lambda j, i: (j, i, 0)),
        out_shape=jax.ShapeDtypeStruct((N // LANES, S, LANES), F32),
        scratch_shapes=[pltpu.VMEM((D, tn), BF16)],
        compiler_params=_params("parallel", "arbitrary"),
        name="inproj",
    )(xw, ssq, w)


def _inproj_first_kernel(x_ref, nw_ref, w_ref, wc_ref, o_ref, wcb_ref, wb):
    @pl.when(pl.program_id(1) == 0)
    def _():
        wb[...] = w_ref[...].astype(BF16)

    wcb_ref[...] = wc_ref[...].astype(BF16)
    x = x_ref[...]
    r = lax.rsqrt(jnp.mean(x * x, axis=-1, keepdims=True) + EPS)
    acc = jnp.dot((x * nw_ref[...]).astype(BF16), wb[...], preferred_element_type=F32)
    for j in range(o_ref.shape[0]):
        o_ref[j] = acc[:, j * LANES:(j + 1) * LANES] * r


def _inproj_first(x, nw, w, l, wc, lc, tm=512, tn=1280):
    S, D = x.shape
    N = w.shape[2]
    nm = S // tm
    wc_in, wc_out, wc_shape = _sidecast(wc, lc, nm, lambda j, i: jnp.minimum(j * nm + i, nm - 1))
    return pl.pallas_call(
        _inproj_first_kernel,
        grid=(N // tn, nm),
        in_specs=[pl.BlockSpec((tm, D), lambda j, i: (i, 0)), pl.BlockSpec((1, D), lambda j, i: (0, 0)),
                  pl.BlockSpec((None, D, tn), lambda j, i: (l, 0, j)), wc_in],
        out_specs=[pl.BlockSpec((tn // LANES, tm, LANES), lambda j, i: (j, i, 0)), wc_out],
        out_shape=[jax.ShapeDtypeStruct((N // LANES, S, LANES), F32), wc_shape],
        scratch_shapes=[pltpu.VMEM((D, tn), BF16)],
        compiler_params=_params("arbitrary", "arbitrary"),
        name="inproj_first",
    )(x, nw.reshape(1, D), w, wc)


def _retention_kernel(q_ref, k_ref, v_ref, g_ref, ra_ref, rb_ref, dec_ref, xi_ref, zeta_ref, cd_ref,
                      nw_ref, o_ref, state_ref, lbuf, ubuf, sbuf):
    @pl.when(pl.program_id(1) == 0)
    def _():
        state_ref[...] = jnp.zeros_like(state_ref)

    scale = LANES ** -0.5
    nh, tr = q_ref.shape[0], q_ref.shape[1]
    nc = tr // CHUNK
    cb, sb = rb_ref[0], rb_ref[1]
    for c in range(nc):
        rows = pl.ds(c * CHUNK, CHUNK)
        ca, sa, cas, sas = (ra_ref[t, pl.ds(c, 1), :] for t in range(4))
        cos = ca * cb - sa * sb
        sin = sas * cb + cas * sb
        for h in range(nh):
            u = c * nh + h
            q, k = q_ref[h, rows, :], k_ref[h, rows, :]
            qr = q * cos + pltpu.roll(q, LANES // 2, 1) * sin
            kr = (k * cos + pltpu.roll(k, LANES // 2, 1) * sin) * scale
            vb = v_ref[h, rows, :].astype(BF16)
            s = lax.dot_general(qr.astype(BF16), kr.astype(BF16), (((1,), (1,)), ((), ())),
                                preferred_element_type=F32) * dec_ref[h]
            lbuf[u, :, 0:CHUNK] = s.astype(BF16)
            lbuf[u, :, CHUNK:CHUNK + LANES] = (qr * xi_ref[h]).astype(BF16)
            ubuf[u] = lax.dot_general((kr * zeta_ref[h]).astype(BF16), vb, (((0,), (0,)), ((), ())),
                                      preferred_element_type=F32)
    for h in range(nh):
        state = state_ref[h]
        for c in range(nc):
            sbuf[c * nh + h] = state.astype(BF16)
            state = cd_ref[h] * state + ubuf[c * nh + h]
        state_ref[h] = state
    for c in range(nc):
        rows = pl.ds(c * CHUNK, CHUNK)
        for h in range(nh):
            u = c * nh + h
            rhs = jnp.concatenate([v_ref[h, rows, :].astype(BF16), sbuf[u]], axis=0)
            o = jnp.dot(lbuf[u], rhs, preferred_element_type=F32)
            o_ref[h, rows, :] = (_silu(g_ref[h, rows, :]) * _rms(o, nw_ref[h])).astype(o_ref.dtype)


def _retention_tables(S):
    half = LANES // 2
    inv_freq = ROPE_BASE ** (-jnp.arange(half, dtype=F32) / half)
    inv_freq = jnp.concatenate([inv_freq, inv_freq])
    sign = jnp.concatenate([-jnp.ones(half, F32), jnp.ones(half, F32)])
    base = jnp.arange(0, S, CHUNK, dtype=jnp.int32).astype(F32)[:, None] * inv_freq[None, :]
    offs = jnp.arange(CHUNK, dtype=jnp.int32).astype(F32)[:, None] * inv_freq[None, :]
    rot_a = jnp.stack([jnp.cos(base), jnp.sin(base), sign * jnp.cos(base), sign * jnp.sin(base)])
    rot_b = jnp.stack([jnp.cos(offs), jnp.sin(offs)])
    log_g = jnp.log1p(-jnp.exp2(-5.0 - jnp.arange(N_RET, dtype=F32)))
    idx = jnp.arange(CHUNK, dtype=F32)
    diff = idx[:, None] - idx[None, :]
    dec = jnp.where(diff[None] >= 0, jnp.exp(jnp.maximum(diff, 0.0)[None] * log_g[:, None, None]), 0.0)
    xi = jnp.exp((idx[None, :] + 1.0) * log_g[:, None])
    zeta = jnp.exp((CHUNK - 1.0 - idx)[None, :] * log_g[:, None])
    cd = jnp.exp(CHUNK * log_g)
    bcast = lambda t: jnp.broadcast_to(t[:, :, None], (N_RET, CHUNK, LANES))
    return rot_a, rot_b, dec, bcast(xi), bcast(zeta), jnp.broadcast_to(cd[:, None, None], (N_RET, 1, LANES))


def _retention(proj, tables, norm_w, tr=1024, nh=N_RET):
    S = proj.shape[1]
    rot_a, rot_b, dec, xi, zeta, cd = tables
    assert all(off % nh == 0 for off in (RQ, RK, RV, RG)) and N_RET % nh == 0
    slab = lambda off: pl.BlockSpec((nh, tr, LANES), lambda h, i: (off // nh + h, i, 0))
    per_head = lambda r: pl.BlockSpec((nh, r, LANES), lambda h, i: (h, 0, 0))
    return pl.pallas_call(
        _retention_kernel,
        grid=(N_RET // nh, S // tr),
        in_specs=[slab(RQ), slab(RK), slab(RV), slab(RG),
                  pl.BlockSpec((4, tr // CHUNK, LANES), lambda h, i: (0, i, 0)),
                  pl.BlockSpec((2, CHUNK, LANES), lambda h, i: (0, 0, 0)),
                  per_head(CHUNK), per_head(CHUNK), per_head(CHUNK), per_head(1), per_head(1)],
        out_specs=pl.BlockSpec((nh, tr, LANES), lambda h, i: (h, i, 0)),
        out_shape=jax.ShapeDtypeStruct((N_RET, S, LANES), BF16),
        scratch_shapes=[pltpu.VMEM((nh, LANES, LANES), F32),
                        pltpu.VMEM((nh * tr // CHUNK, CHUNK, CHUNK + LANES), BF16),
                        pltpu.VMEM((nh * tr // CHUNK, LANES, LANES), F32),
                        pltpu.VMEM((nh * tr // CHUNK, LANES, LANES), BF16)],
        compiler_params=_params("parallel", "arbitrary"),
        name="retention",
    )(proj, proj, proj, proj, rot_a, rot_b, dec, xi, zeta, cd, norm_w.reshape(N_RET, 1, LANES))


D2 = DILATIONS[1]
TROWS = SPAN // D2
UNITS = 3 * (SPAN // CHUNK)
assert DILATIONS == (1, D2, D2 * D2) and SPAN == CHUNK * D2 * D2


def _dilated_kernel(q_ref, k_ref, v_ref, o_ref, tq, tk, tv, kt, vt, sbuf, o1, d1, m1, o2, d2, m2, o3, d3, m3):
    n = pl.program_id(1)
    slot = n % 2
    pslot = 1 - slot
    nblk = SPAN // CHUNK
    scale2 = LANES ** -0.5 * math.log2(math.e)
    blk = lambda i: pl.ds(i * CHUNK, CHUNK)
    strided = lambda start: pl.ds(start, CHUNK, stride=D2)

    @pl.when(n == 0)
    def _():
        tk[1] = jnp.zeros(tk.shape[1:], F32)
        tv[1] = jnp.zeros(tv.shape[1:], F32)
        kt[...] = jnp.zeros(kt.shape, F32)
        vt[...] = jnp.zeros(vt.shape, F32)

    for r in range(D2):
        tq[r] = q_ref[pl.ds(r, TROWS, stride=D2), :]
        tk[slot, r] = k_ref[pl.ds(r, TROWS, stride=D2), :]
        tv[slot, r] = v_ref[pl.ds(r, TROWS, stride=D2), :]

    qi = lax.broadcasted_iota(jnp.int32, (CHUNK, CHUNK), 0)
    ci = lax.broadcasted_iota(jnp.int32, (CHUNK, CHUNK), 1)
    own_ok = ci <= qi
    prev_ok = ci >= qi
    prev_ok_span = ci >= qi + jnp.where(n > 0, 0, CHUNK)

    def units(it):
        ss, r = it // D2, it % D2
        narrow_prev = (lambda kv: kv[2][...]) if it == 0 else (lambda kv: kv[0][blk(it - 1), :])
        mid_prev = ((lambda kv: kv[1][slot, r, blk(ss - 1), :]) if ss > 0 else
                    (lambda kv: kv[1][pslot, r, blk(D2 - 1), :]))
        return (
            (lambda: q_ref[blk(it), :], narrow_prev, lambda kv: kv[0][blk(it), :],
             prev_ok if it > 0 else prev_ok_span, (blk(it), slice(None))),
            (lambda: tq[r, blk(ss), :], mid_prev, lambda kv: kv[1][slot, r, blk(ss), :],
             prev_ok if ss > 0 else prev_ok_span, (r, blk(ss), slice(None))),
            (lambda: tq[r, strided(ss), :], lambda kv: kv[1][pslot, r, strided(ss), :],
             lambda kv: kv[1][slot, r, strided(ss), :], prev_ok_span, (r, strided(ss), slice(None))),
        )

    keys = (k_ref, tk, kt)
    vals = (v_ref, tv, vt)

    for it in range(nblk):
        for p, (load_q, load_prev, load_own, mask_prev, rows), m_s in zip(range(3), units(it), (m1, m2, m3)):
            k = jnp.concatenate([load_prev(keys), load_own(keys)], axis=0).astype(BF16)
            s = lax.dot_general(load_q().astype(BF16), k, (((1,), (1,)), ((), ())),
                                preferred_element_type=F32)
            sp = jnp.where(mask_prev, s[:, :CHUNK], NEG_INF)
            so = jnp.where(own_ok, s[:, CHUNK:], NEG_INF)
            sbuf[3 * it + p, :, 0:CHUNK] = sp
            sbuf[3 * it + p, :, CHUNK:2 * CHUNK] = so
            m_s[rows] = jnp.broadcast_to(jnp.max(jnp.maximum(sp, so), axis=-1, keepdims=True), (CHUNK, LANES))

    ones = jnp.ones((2 * CHUNK, LANES), BF16)
    for it in range(nblk):
        for p, (_, load_prev, load_own, _, rows), o_s, d_s, m_s in zip(
                range(3), units(it), (o1, o2, o3), (d1, d2, d3), (m1, m2, m3)):
            m = m_s[rows]
            e = jnp.concatenate([jnp.exp2((sbuf[3 * it + p, :, 0:CHUNK] - m) * scale2),
                                 jnp.exp2((sbuf[3 * it + p, :, CHUNK:2 * CHUNK] - m) * scale2)], axis=1)
            v = jnp.concatenate([load_prev(vals), load_own(vals)], axis=0).astype(BF16)
            oa = jnp.dot(e.astype(BF16), jnp.concatenate([v, ones], axis=1), preferred_element_type=F32)
            o_s[rows] = oa[:, :LANES]
            d_s[rows] = oa[:, LANES:]

    kt[...] = k_ref[blk(nblk - 1), :]
    vt[...] = v_ref[blk(nblk - 1), :]

    def phase_c(j, carry):
        for r in range(D2):
            nat = (pl.ds(j * CHUNK * D2 + r, CHUNK, stride=D2), slice(None))
            trw = (r, pl.ds(j * CHUNK, CHUNK), slice(None))
            ma, mb, mc = m1[nat], m2[trw], m3[trw]
            mx = jnp.maximum(jnp.maximum(ma, mb), mc)
            wa, wb, wc = (jnp.exp2((t - mx) * scale2) for t in (ma, mb, mc))
            num = wa * o1[nat] + wb * o2[trw] + wc * o3[trw]
            den = wa * d1[nat] + wb * d2[trw] + wc * d3[trw]
            o_ref[nat] = num / den
        return carry

    lax.fori_loop(0, TROWS // CHUNK, phase_c, 0)


def _dilated_attention(proj):
    S = proj.shape[1]
    slab = lambda off: pl.BlockSpec((None, SPAN, LANES), lambda h, n: (off + h, n, 0))
    nat = pltpu.VMEM((SPAN, LANES), F32)
    res = pltpu.VMEM((D2, TROWS, LANES), F32)
    ring = pltpu.VMEM((2, D2, TROWS, LANES), F32)
    tail = pltpu.VMEM((CHUNK, LANES), F32)
    return pl.pallas_call(
        _dilated_kernel,
        grid=(N_ATT, S // SPAN),
        in_specs=[slab(AQ), slab(AK), slab(AV)],
        out_specs=pl.BlockSpec((None, SPAN, LANES), lambda h, n: (h, n, 0)),
        out_shape=jax.ShapeDtypeStruct((N_ATT, S, LANES), F32),
        scratch_shapes=[res, ring, ring, tail, tail, pltpu.VMEM((UNITS, CHUNK, 2 * CHUNK), F32),
                        nat, nat, nat, res, res, res, res, res, res],
        compiler_params=_params("parallel", "arbitrary"),
        name="dilated_attention",
    )(proj, proj, proj)


def _gmlp_kernel(*refs):
    G = N_GMLP
    u_refs, v_refs = refs[:G], refs[G:2 * G]
    lnw_ref, ws_ref, bs_ref, ow_ref, o_ref = refs[2 * G:]
    tm = o_ref.shape[1]
    width = G * LANES
    gv = [jax.nn.gelu(r[...]) for r in v_refs]
    mu = sum(jnp.sum(t, axis=-1, keepdims=True) for t in gv) / width
    xc = [t - mu for t in gv]
    inv = lax.rsqrt(sum(jnp.sum(t * t, axis=-1, keepdims=True) for t in xc) / width + EPS)
    qi = lax.broadcasted_iota(jnp.int32, (CHUNK, CHUNK), 0)
    ki = lax.broadcasted_iota(jnp.int32, (CHUNK, CHUNK), 1)
    outs = []
    for g in range(G):
        vn = (xc[g] * inv * lnw_ref[g]).astype(BF16)
        wm = jnp.where(ki <= qi, ws_ref[g], 0.0).astype(BF16)
        sp = [jnp.dot(wm, vn[c * CHUNK:(c + 1) * CHUNK, :], preferred_element_type=F32) + bs_ref[g]
              for c in range(tm // CHUNK)]
        outs.append(jax.nn.gelu(u_refs[g][...]) * jnp.concatenate(sp, axis=0))
    inv_o = lax.rsqrt(sum(jnp.sum(t * t, axis=-1, keepdims=True) for t in outs) / width + EPS)
    for g in range(G):
        o_ref[g] = (outs[g] * inv_o * ow_ref[g]).astype(o_ref.dtype)


def _gmlp(proj, ln_w, ws, bs, out_w, tm=1024):
    S = proj.shape[1]
    G = N_GMLP
    slab = lambda s: pl.BlockSpec((None, tm, LANES), lambda i: (s, i, 0))
    small = lambda shape: pl.BlockSpec(shape, lambda i: (0,) * len(shape))
    return pl.pallas_call(
        _gmlp_kernel,
        grid=(S // tm,),
        in_specs=[slab(GU + g) for g in range(G)] + [slab(GV + g) for g in range(G)]
        + [small((G, 1, LANES)), small((G, CHUNK, CHUNK)), small((G, CHUNK, 1)), small((G, 1, LANES))],
        out_specs=pl.BlockSpec((G, tm, LANES), lambda i: (0, i, 0)),
        out_shape=jax.ShapeDtypeStruct((G, S, LANES), BF16),
        compiler_params=_params("parallel"),
        name="gmlp",
    )(*([proj] * (2 * G)), ln_w.reshape(G, 1, LANES), ws, bs.reshape(G, CHUNK, 1), out_w.reshape(G, 1, LANES))


def _outproj_kernel(ret_ref, att_ref, gm_ref, x_ref, attw_ref, wb, n2w_ref, x1_ref, h2_ref, a_scr, *, nc):
    col = lambda j: slice(j * LANES, (j + 1) * LANES)
    for j in range(N_RET):
        a_scr[:, col(j)] = ret_ref[j].astype(BF16)
    att = [att_ref[j] for j in range(N_ATT)]
    inv = lax.rsqrt(sum(jnp.sum(t * t, axis=-1, keepdims=True) for t in att) / (N_ATT * LANES) + EPS)
    for j in range(N_ATT):
        a_scr[:, col(N_RET + j)] = (att[j] * inv * attw_ref[j]).astype(BF16)
    for j in range(N_GMLP):
        a_scr[:, col(N_RET + N_ATT + j)] = gm_ref[j].astype(BF16)
    D = x_ref.shape[1]
    ssq = 0.0
    for c in range(D // nc):
        cs = slice(c * nc, (c + 1) * nc)
        x1 = x_ref[:, cs] + jnp.dot(a_scr[...], wb[:, cs], preferred_element_type=F32)
        x1_ref[:, cs] = x1
        ssq = ssq + jnp.sum(x1 * x1, axis=-1, keepdims=True)
    h2_ref[...] = (x1_ref[...] * lax.rsqrt(ssq / D + EPS) * n2w_ref[...]).astype(BF16)


def _outproj(ret, att, gm, x, att_w, w, n2w, tm=512, nc=512):
    S, D = x.shape
    K = w.shape[0]
    rows = lambda n: pl.BlockSpec((n, tm, LANES), lambda i: (0, i, 0))
    full = lambda: pl.BlockSpec((tm, D), lambda i: (i, 0))
    const = lambda shape: pl.BlockSpec(shape, lambda i: (0,) * len(shape))
    return pl.pallas_call(
        functools.partial(_outproj_kernel, nc=nc),
        grid=(S // tm,),
        in_specs=[rows(N_RET), rows(N_ATT), rows(N_GMLP), full(), const((N_ATT, 1, LANES)),
                  pl.BlockSpec((K, D), lambda i: (0, 0), pipeline_mode=pl.Buffered(1)), const((1, D))],
        out_specs=[full(), full()],
        out_shape=[jax.ShapeDtypeStruct((S, D), F32), jax.ShapeDtypeStruct((S, D), BF16)],
        scratch_shapes=[pltpu.VMEM((tm, K), BF16)],
        compiler_params=_params("parallel"),
        name="outproj",
    )(ret, att, gm, x, att_w.reshape(N_ATT, 1, LANES), w, n2w.reshape(1, D))


def _upproj_kernel(a_ref, wg_ref, wv_ref, cwg_ref, cwv_ref, cbg_ref, cbv_ref, wc_ref, o_ref, wcb_ref, wgb, wvb, hg, hv):
    wcb_ref[...] = wc_ref[...].astype(BF16)
    tm = a_ref.shape[0]
    nslab = hg.shape[0]
    head = SUBLANES

    @pl.when(pl.program_id(1) == 0)
    def _():
        wgb[...] = wg_ref[...].astype(BF16)
        wvb[...] = wv_ref[...].astype(BF16)
        hg[:, 0:head, :] = jnp.zeros((nslab, head, LANES), F32)
        hv[:, 0:head, :] = jnp.zeros((nslab, head, LANES), F32)

    a = a_ref[...]

    def conv(wb, cw_ref, cb_ref, hbuf):
        h = jnp.dot(a, wb[...], preferred_element_type=F32)
        outs = []
        for s in range(nslab):
            cs = slice(s * LANES, (s + 1) * LANES)
            hs = h[:, cs]
            hbuf[s, head:head + tm, :] = hs
            outs.append(cb_ref[:, cs] + cw_ref[0:1, cs] * hbuf[s, head - 2:head - 2 + tm, :]
                        + cw_ref[1:2, cs] * hbuf[s, head - 1:head - 1 + tm, :] + cw_ref[2:3, cs] * hs)
            hbuf[s, 0:head, :] = hbuf[s, tm:tm + head, :]
        return outs

    gate = conv(wgb, cwg_ref, cbg_ref, hg)
    val = conv(wvb, cwv_ref, cbv_ref, hv)
    for s in range(nslab):
        o_ref[:, s * LANES:(s + 1) * LANES] = (_silu(gate[s]) * val[s]).astype(o_ref.dtype)


def _upproj(h2, w_up, conv_w, conv_b, wc, l, tm=1024, tn=512):
    S, D = h2.shape
    F = w_up.shape[2] // 2
    nj, nm = F // tn, S // tm
    taps = conv_w.shape[1]
    hbuf = pltpu.VMEM((tn // LANES, tm + SUBLANES, LANES), F32)
    wc_in, wc_out, wc_shape = _sidecast(wc, l, nj * nm, lambda j, i: j * nm + i)
    return pl.pallas_call(
        _upproj_kernel,
        grid=(nj, nm),
        in_specs=[pl.BlockSpec((tm, D), lambda j, i: (i, 0)),
                  pl.BlockSpec((None, D, tn), lambda j, i: (l, 0, j)),
                  pl.BlockSpec((None, D, tn), lambda j, i: (l, 0, j + nj)),
                  pl.BlockSpec((None, taps, tn), lambda j, i: (l, 0, j)),
                  pl.BlockSpec((None, taps, tn), lambda j, i: (l, 0, j + nj)),
                  pl.BlockSpec((None, 1, tn), lambda j, i: (l, 0, j)),
                  pl.BlockSpec((None, 1, tn), lambda j, i: (l, 0, j + nj)),
                  wc_in],
        out_specs=[pl.BlockSpec((tm, tn), lambda j, i: (i, j)), wc_out],
        out_shape=[jax.ShapeDtypeStruct((S, F), BF16), wc_shape],
        scratch_shapes=[pltpu.VMEM((D, tn), BF16), pltpu.VMEM((D, tn), BF16), hbuf, hbuf],
        compiler_params=_params("parallel", "arbitrary"),
        name="upproj",
    )(h2, w_up, w_up, conv_w, conv_w, conv_b[:, None, :], conv_b[:, None, :], wc)


def _downproj_kernel(a_ref, w_ref, x1_ref, nw_ref, wc_ref, o_ref, xw_ref, ssq_ref, wcb_ref):
    x2 = x1_ref[...] + jnp.dot(a_ref[...], w_ref[...], preferred_element_type=F32)
    o_ref[...] = x2
    xw_ref[...] = (x2 * nw_ref[...]).astype(BF16)
    ssq_ref[...] = jnp.sum(x2 * x2, axis=-1, keepdims=True)
    wcb_ref[...] = wc_ref[...].astype(BF16)


def _downproj(g, w, x1, next_w, wc, lc, tm=512, tn=1024):
    S, F = g.shape
    D = w.shape[1]
    nm = S // tm
    tile = lambda: pl.BlockSpec((tm, tn), lambda j, i: (i, j))
    wc_in, wc_out, wc_shape = _sidecast(wc, lc, (D // tn) * nm, lambda j, i: j * nm + i)
    return pl.pallas_call(
        _downproj_kernel,
        grid=(D // tn, nm),
        in_specs=[pl.BlockSpec((tm, F), lambda j, i: (i, 0)), pl.BlockSpec((F, tn), lambda j, i: (0, j)), tile(),
                  pl.BlockSpec((1, tn), lambda j, i: (0, j)), wc_in],
        out_specs=[tile(), tile(), pl.BlockSpec((None, tm, 1), lambda j, i: (j, i, 0)), wc_out],
        out_shape=[jax.ShapeDtypeStruct((S, D), F32), jax.ShapeDtypeStruct((S, D), BF16),
                   jax.ShapeDtypeStruct((D // tn, S, 1), F32), wc_shape],
        compiler_params=_params("parallel", "parallel"),
        name="downproj",
    )(g, w, x1, next_w.reshape(1, D), wc)


def _downproj_final_kernel(a_ref, w_ref, x1_ref, nw_ref, o_ref, *, nc):
    D = o_ref.shape[1]
    a = a_ref[...]
    ssq = 0.0
    for c in range(D // nc):
        cs = slice(c * nc, (c + 1) * nc)
        x2 = x1_ref[:, cs] + jnp.dot(a, w_ref[:, cs], preferred_element_type=F32)
        o_ref[:, cs] = x2
        ssq = ssq + jnp.sum(x2 * x2, axis=-1, keepdims=True)
    o_ref[...] = o_ref[...] * lax.rsqrt(ssq / D + EPS) * nw_ref[...]


def _downproj_final(g, w, x1, final_w, tm=512, nc=512):
    S, F = g.shape
    D = w.shape[1]
    rows = lambda: pl.BlockSpec((tm, D), lambda i: (i, 0))
    return pl.pallas_call(
        functools.partial(_downproj_final_kernel, nc=nc),
        grid=(S // tm,),
        in_specs=[pl.BlockSpec((tm, F), lambda i: (i, 0)),
                  pl.BlockSpec((F, D), lambda i: (0, 0), pipeline_mode=pl.Buffered(1)),
                  rows(), pl.BlockSpec((1, D), lambda i: (0, 0))],
        out_specs=rows(),
        out_shape=jax.ShapeDtypeStruct((S, D), F32),
        compiler_params=_params("parallel"),
        name="downproj_final",
    )(g, w, x1, final_w.reshape(1, D))


def kernel(x, norm1_w, w_in, ret_norm_w, att_norm_w, gmlp_ln_w, gmlp_ws, gmlp_bs, gmlp_out_w, w_out, norm2_w, w_up, conv_w, conv_b, w_down, final_norm_w):
    B, S, D = x.shape
    depth = w_in.shape[0]
    assert B == 1 and S % SPAN == 0 and w_in.shape[2] == N_SLABS * LANES
    tables = _retention_tables(S)
    xs = x.reshape(S, D)
    for l in range(depth):
        if l == 0:
            proj, w_out_b = _inproj_first(xs, norm1_w[0], w_in, 0, w_out, 0)
        else:
            proj = _inproj(xw, ssq, w_in, l)
        ret = _retention(proj, tables, ret_norm_w[l])
        att = _dilated_attention(proj)
        gm = _gmlp(proj, gmlp_ln_w[l], gmlp_ws[l], gmlp_bs[l], gmlp_out_w[l])
        x1, h2 = _outproj(ret, att, gm, xs, att_norm_w[l], w_out_b, norm2_w[l])
        g, w_down_b = _upproj(h2, w_up, conv_w, conv_b, w_down, l)
        if l + 1 < depth:
            xs, xw, ssq, w_out_b = _downproj(g, w_down_b, x1, norm1_w[l + 1], w_out, l + 1)
        else:
            out = _downproj_final(g, w_down_b, x1, final_norm_w)
    return out.reshape(B, S, D)
```

```python
import functools
import math

import jax
import jax.numpy as jnp
from jax import lax
from jax.experimental import pallas as pl
from jax.experimental.pallas import tpu as pltpu

F32 = jnp.float32
BF16 = jnp.bfloat16

LANES = 128
SUBLANES = 8
BF16_SUBLANES = 16
CHUNK = 128
N_RET = 6
N_ATT = 6
N_GMLP = 4
DILATIONS = (1, 4, 16)
SPAN = CHUNK * DILATIONS[-1]
ROPE_BASE = 10000.0
EPS = 1e-6
NEG_INF = -1e30
VMEM_LIMIT = 56 * 1024 * 1024

RQ, RK, RV, RG = 0, N_RET, 2 * N_RET, 3 * N_RET
AQ = 4 * N_RET
AK = AQ + N_ATT
AV = AK + N_ATT
GU = AV + N_ATT
GV = GU + N_GMLP
N_SLABS = GV + N_GMLP


def _params(*sem):
    return pltpu.CompilerParams(dimension_semantics=sem, vmem_limit_bytes=VMEM_LIMIT)


def _sidecast(w, l, steps, step_of):
    _, rows, cols = w.shape
    band = rows // steps
    assert band * steps == rows and band % BF16_SUBLANES == 0
    return (pl.BlockSpec((None, band, cols), lambda *g: (l, step_of(*g), 0)),
            pl.BlockSpec((band, cols), lambda *g: (step_of(*g), 0)),
            jax.ShapeDtypeStruct((rows, cols), BF16))


def _rms(x, w):
    return x * lax.rsqrt(jnp.mean(x * x, axis=-1, keepdims=True) + EPS) * w


def _silu(x):
    hx = 0.5 * x
    return hx + hx * jnp.tanh(hx)


def _inproj_kernel(a_ref, ssq_ref, w_ref, o_ref, wb):
    @pl.when(pl.program_id(1) == 0)
    def _():
        wb[...] = w_ref[...].astype(BF16)

    D = a_ref.shape[1]
    r = lax.rsqrt(sum(ssq_ref[p] for p in range(ssq_ref.shape[0])) / D + EPS)
    acc = jnp.dot(a_ref[...], wb[...], preferred_element_type=F32)
    for j in range(o_ref.shape[0]):
        o_ref[j] = (acc[:, j * LANES:(j + 1) * LANES] * r).astype(o_ref.dtype)


def _inproj(xw, ssq, w, l, tm=1024, tn=1280):
    S, D = xw.shape
    N = w.shape[2]
    parts = ssq.shape[0]
    return pl.pallas_call(
        _inproj_kernel,
        grid=(N // tn, S // tm),
        in_specs=[pl.BlockSpec((tm, D), lambda j, i: (i, 0)), pl.BlockSpec((parts, tm, 1), lambda j, i: (0, i, 0)),
                  pl.BlockSpec((None, D, tn), lambda j, i: (l, 0, j))],
        out_specs=pl.BlockSpec((tn // LANES, tm, LANES), lambda j, i: (j, i, 0)),
        out_shape=jax.ShapeDtypeStruct((N // LANES, S, LANES), BF16),
        scratch_shapes=[pltpu.VMEM((D, tn), BF16)],
        compiler_params=_params("parallel", "arbitrary"),
        name="inproj",
    )(xw, ssq, w)


def _inproj_first_kernel(x_ref, nw_ref, w_ref, wc_ref, o_ref, wcb_ref, wb):
    @pl.when(pl.program_id(1) == 0)
    def _():
        wb[...] = w_ref[...].astype(BF16)

    wcb_ref[...] = wc_ref[...].astype(BF16)
    x = x_ref[...]
    r = lax.rsqrt(jnp.mean(x * x, axis=-1, keepdims=True) + EPS)
    acc = jnp.dot((x * nw_ref[...]).astype(BF16), wb[...], preferred_element_type=F32)
    for j in range(o_ref.shape[0]):
        o_ref[j] = (acc[:, j * LANES:(j + 1) * LANES] * r).astype(o_ref.dtype)


def _inproj_first(x, nw, w, l, wc, lc, tm=512, tn=1280):
    S, D = x.shape
    N = w.shape[2]
    nm = S // tm
    wc_in, wc_out, wc_shape = _sidecast(wc, lc, nm, lambda j, i: jnp.minimum(j * nm + i, nm - 1))
    return pl.pallas_call(
        _inproj_first_kernel,
        grid=(N // tn, nm),
        in_specs=[pl.BlockSpec((tm, D), lambda j, i: (i, 0)), pl.BlockSpec((1, D), lambda j, i: (0, 0)),
                  pl.BlockSpec((None, D, tn), lambda j, i: (l, 0, j)), wc_in],
        out_specs=[pl.BlockSpec((tn // LANES, tm, LANES), lambda j, i: (j, i, 0)), wc_out],
        out_shape=[jax.ShapeDtypeStruct((N // LANES, S, LANES), BF16), wc_shape],
        scratch_shapes=[pltpu.VMEM((D, tn), BF16)],
        compiler_params=_params("arbitrary", "arbitrary"),
        name="inproj_first",
    )(x, nw.reshape(1, D), w, wc)


def _retention_kernel(q_ref, k_ref, v_ref, g_ref, ra_ref, rb_ref, dec_ref, xi_ref, zeta_ref, cd_ref,
                      nw_ref, o_ref, state_ref, lbuf, ubuf, sbuf):
    @pl.when(pl.program_id(1) == 0)
    def _():
        state_ref[...] = jnp.zeros_like(state_ref)

    scale = LANES ** -0.5
    nh, tr = q_ref.shape[0], q_ref.shape[1]
    nc = tr // CHUNK
    cb, sb = rb_ref[0], rb_ref[1]
    for c in range(nc):
        rows = pl.ds(c * CHUNK, CHUNK)
        ca, sa, cas, sas = (ra_ref[t, pl.ds(c, 1), :] for t in range(4))
        cos = ca * cb - sa * sb
        sin = sas * cb + cas * sb
        for h in range(nh):
            u = c * nh + h
            q, k = q_ref[h, rows, :].astype(F32), k_ref[h, rows, :].astype(F32)
            qr = q * cos + pltpu.roll(q, LANES // 2, 1) * sin
            kr = (k * cos + pltpu.roll(k, LANES // 2, 1) * sin) * scale
            vb = v_ref[h, rows, :].astype(BF16)
            s = lax.dot_general(qr.astype(BF16), kr.astype(BF16), (((1,), (1,)), ((), ())),
                                preferred_element_type=F32) * dec_ref[h]
            lbuf[u, :, 0:CHUNK] = s.astype(BF16)
            lbuf[u, :, CHUNK:CHUNK + LANES] = (qr * xi_ref[h]).astype(BF16)
            ubuf[u] = lax.dot_general((kr * zeta_ref[h]).astype(BF16), vb, (((0,), (0,)), ((), ())),
                                      preferred_element_type=F32)
    for h in range(nh):
        state = state_ref[h]
        for c in range(nc):
            sbuf[c * nh + h] = state.astype(BF16)
            state = cd_ref[h] * state + ubuf[c * nh + h]
        state_ref[h] = state
    for c in range(nc):
        rows = pl.ds(c * CHUNK, CHUNK)
        for h in range(nh):
            u = c * nh + h
            rhs = jnp.concatenate([v_ref[h, rows, :].astype(BF16), sbuf[u]], axis=0)
            o = jnp.dot(lbuf[u], rhs, preferred_element_type=F32)
            o_ref[h, rows, :] = (_silu(g_ref[h, rows, :].astype(F32)) * _rms(o, nw_ref[h])).astype(o_ref.dtype)


def _retention_tables(S):
    half = LANES // 2
    inv_freq = ROPE_BASE ** (-jnp.arange(half, dtype=F32) / half)
    inv_freq = jnp.concatenate([inv_freq, inv_freq])
    sign = jnp.concatenate([-jnp.ones(half, F32), jnp.ones(half, F32)])
    base = jnp.arange(0, S, CHUNK, dtype=jnp.int32).astype(F32)[:, None] * inv_freq[None, :]
    offs = jnp.arange(CHUNK, dtype=jnp.int32).astype(F32)[:, None] * inv_freq[None, :]
    rot_a = jnp.stack([jnp.cos(base), jnp.sin(base), sign * jnp.cos(base), sign * jnp.sin(base)])
    rot_b = jnp.stack([jnp.cos(offs), jnp.sin(offs)])
    log_g = jnp.log1p(-jnp.exp2(-5.0 - jnp.arange(N_RET, dtype=F32)))
    idx = jnp.arange(CHUNK, dtype=F32)
    diff = idx[:, None] - idx[None, :]
    dec = jnp.where(diff[None] >= 0, jnp.exp(jnp.maximum(diff, 0.0)[None] * log_g[:, None, None]), 0.0)
    xi = jnp.exp((idx[None, :] + 1.0) * log_g[:, None])
    zeta = jnp.exp((CHUNK - 1.0 - idx)[None, :] * log_g[:, None])
    cd = jnp.exp(CHUNK * log_g)
    bcast = lambda t: jnp.broadcast_to(t[:, :, None], (N_RET, CHUNK, LANES))
    return rot_a, rot_b, dec, bcast(xi), bcast(zeta), jnp.broadcast_to(cd[:, None, None], (N_RET, 1, LANES))


def _retention(proj, tables, norm_w, tr=1024, nh=N_RET):
    S = proj.shape[1]
    rot_a, rot_b, dec, xi, zeta, cd = tables
    assert all(off % nh == 0 for off in (RQ, RK, RV, RG)) and N_RET % nh == 0
    slab = lambda off: pl.BlockSpec((nh, tr, LANES), lambda h, i: (off // nh + h, i, 0))
    per_head = lambda r: pl.BlockSpec((nh, r, LANES), lambda h, i: (h, 0, 0))
    return pl.pallas_call(
        _retention_kernel,
        grid=(N_RET // nh, S // tr),
        in_specs=[slab(RQ), slab(RK), slab(RV), slab(RG),
                  pl.BlockSpec((4, tr // CHUNK, LANES), lambda h, i: (0, i, 0)),
                  pl.BlockSpec((2, CHUNK, LANES), lambda h, i: (0, 0, 0)),
                  per_head(CHUNK), per_head(CHUNK), per_head(CHUNK), per_head(1), per_head(1)],
        out_specs=pl.BlockSpec((nh, tr, LANES), lambda h, i: (h, i, 0)),
        out_shape=jax.ShapeDtypeStruct((N_RET, S, LANES), BF16),
        scratch_shapes=[pltpu.VMEM((nh, LANES, LANES), F32),
                        pltpu.VMEM((nh * tr // CHUNK, CHUNK, CHUNK + LANES), BF16),
                        pltpu.VMEM((nh * tr // CHUNK, LANES, LANES), F32),
                        pltpu.VMEM((nh * tr // CHUNK, LANES, LANES), BF16)],
        compiler_params=_params("parallel", "arbitrary"),
        name="retention",
    )(proj, proj, proj, proj, rot_a, rot_b, dec, xi, zeta, cd, norm_w.reshape(N_RET, 1, LANES))


D2 = DILATIONS[1]
TROWS = SPAN // D2
UNITS = 3 * (SPAN // CHUNK)
assert DILATIONS == (1, D2, D2 * D2) and SPAN == CHUNK * D2 * D2


def _dilated_kernel(q_in, k_in, v_in, o_ref, q_ref, k_ref, v_ref, tq, tk, tv, kt, vt, sbuf,
                    o1, d1, m1, o2, d2, m2, o3, d3, m3):
    n = pl.program_id(1)
    slot = n % 2
    pslot = 1 - slot
    nblk = SPAN // CHUNK
    scale2 = LANES ** -0.5 * math.log2(math.e)
    blk = lambda i: pl.ds(i * CHUNK, CHUNK)
    strided = lambda start: pl.ds(start, CHUNK, stride=D2)

    @pl.when(n == 0)
    def _():
        tk[1] = jnp.zeros(tk.shape[1:], F32)
        tv[1] = jnp.zeros(tv.shape[1:], F32)
        kt[...] = jnp.zeros(kt.shape, F32)
        vt[...] = jnp.zeros(vt.shape, F32)

    q_ref[...] = q_in[...].astype(F32)
    k_ref[...] = k_in[...].astype(F32)
    v_ref[...] = v_in[...].astype(F32)
    for r in range(D2):
        tq[r] = q_ref[pl.ds(r, TROWS, stride=D2), :]
        tk[slot, r] = k_ref[pl.ds(r, TROWS, stride=D2), :]
        tv[slot, r] = v_ref[pl.ds(r, TROWS, stride=D2), :]

    qi = lax.broadcasted_iota(jnp.int32, (CHUNK, CHUNK), 0)
    ci = lax.broadcasted_iota(jnp.int32, (CHUNK, CHUNK), 1)
    own_ok = ci <= qi
    prev_ok = ci >= qi
    prev_ok_span = ci >= qi + jnp.where(n > 0, 0, CHUNK)

    def units(it):
        ss, r = it // D2, it % D2
        narrow_prev = (lambda kv: kv[2][...]) if it == 0 else (lambda kv: kv[0][blk(it - 1), :])
        mid_prev = ((lambda kv: kv[1][slot, r, blk(ss - 1), :]) if ss > 0 else
                    (lambda kv: kv[1][pslot, r, blk(D2 - 1), :]))
        return (
            (lambda: q_ref[blk(it), :], narrow_prev, lambda kv: kv[0][blk(it), :],
             prev_ok if it > 0 else prev_ok_span, (blk(it), slice(None))),
            (lambda: tq[r, blk(ss), :], mid_prev, lambda kv: kv[1][slot, r, blk(ss), :],
             prev_ok if ss > 0 else prev_ok_span, (r, blk(ss), slice(None))),
            (lambda: tq[r, strided(ss), :], lambda kv: kv[1][pslot, r, strided(ss), :],
             lambda kv: kv[1][slot, r, strided(ss), :], prev_ok_span, (r, strided(ss), slice(None))),
        )

    keys = (k_ref, tk, kt)
    vals = (v_ref, tv, vt)

    for it in range(nblk):
        for p, (load_q, load_prev, load_own, mask_prev, rows), m_s in zip(range(3), units(it), (m1, m2, m3)):
            k = jnp.concatenate([load_prev(keys), load_own(keys)], axis=0).astype(BF16)
            s = lax.dot_general(load_q().astype(BF16), k, (((1,), (1,)), ((), ())),
                                preferred_element_type=F32)
            sp = jnp.where(mask_prev, s[:, :CHUNK], NEG_INF)
            so = jnp.where(own_ok, s[:, CHUNK:], NEG_INF)
            sbuf[3 * it + p, :, 0:CHUNK] = sp
            sbuf[3 * it + p, :, CHUNK:2 * CHUNK] = so
            m_s[rows] = jnp.broadcast_to(jnp.max(jnp.maximum(sp, so), axis=-1, keepdims=True), (CHUNK, LANES))

    ones = jnp.ones((2 * CHUNK, LANES), BF16)
    for it in range(nblk):
        for p, (_, load_prev, load_own, _, rows), o_s, d_s, m_s in zip(
                range(3), units(it), (o1, o2, o3), (d1, d2, d3), (m1, m2, m3)):
            m = m_s[rows]
            e = jnp.concatenate([jnp.exp2((sbuf[3 * it + p, :, 0:CHUNK] - m) * scale2),
                                 jnp.exp2((sbuf[3 * it + p, :, CHUNK:2 * CHUNK] - m) * scale2)], axis=1)
            v = jnp.concatenate([load_prev(vals), load_own(vals)], axis=0).astype(BF16)
            oa = jnp.dot(e.astype(BF16), jnp.concatenate([v, ones], axis=1), preferred_element_type=F32)
            o_s[rows] = oa[:, :LANES]
            d_s[rows] = oa[:, LANES:]

    kt[...] = k_ref[blk(nblk - 1), :]
    vt[...] = v_ref[blk(nblk - 1), :]

    def phase_c(j, carry):
        for r in range(D2):
            nat = (pl.ds(j * CHUNK * D2 + r, CHUNK, stride=D2), slice(None))
            trw = (r, pl.ds(j * CHUNK, CHUNK), slice(None))
            ma, mb, mc = m1[nat], m2[trw], m3[trw]
            mx = jnp.maximum(jnp.maximum(ma, mb), mc)
            wa, wb, wc = (jnp.exp2((t - mx) * scale2) for t in (ma, mb, mc))
            num = wa * o1[nat] + wb * o2[trw] + wc * o3[trw]
            den = wa * d1[nat] + wb * d2[trw] + wc * d3[trw]
            o_ref[nat] = num / den
        return carry

    lax.fori_loop(0, TROWS // CHUNK, phase_c, 0)


def _dilated_attention(proj):
    S = proj.shape[1]
    slab = lambda off: pl.BlockSpec((None, SPAN, LANES), lambda h, n: (off + h, n, 0))
    nat = pltpu.VMEM((SPAN, LANES), F32)
    res = pltpu.VMEM((D2, TROWS, LANES), F32)
    ring = pltpu.VMEM((2, D2, TROWS, LANES), F32)
    tail = pltpu.VMEM((CHUNK, LANES), F32)
    return pl.pallas_call(
        _dilated_kernel,
        grid=(N_ATT, S // SPAN),
        in_specs=[slab(AQ), slab(AK), slab(AV)],
        out_specs=pl.BlockSpec((None, SPAN, LANES), lambda h, n: (h, n, 0)),
        out_shape=jax.ShapeDtypeStruct((N_ATT, S, LANES), F32),
        scratch_shapes=[nat, nat, nat, res, ring, ring, tail, tail, pltpu.VMEM((UNITS, CHUNK, 2 * CHUNK), F32),
                        nat, nat, nat, res, res, res, res, res, res],
        compiler_params=_params("parallel", "arbitrary"),
        name="dilated_attention",
    )(proj, proj, proj)


def _gmlp_kernel(*refs):
    G = N_GMLP
    u_refs, v_refs = refs[:G], refs[G:2 * G]
    lnw_ref, ws_ref, bs_ref, ow_ref, o_ref = refs[2 * G:]
    tm = o_ref.shape[1]
    width = G * LANES
    gv = [jax.nn.gelu(r[...].astype(F32)) for r in v_refs]
    mu = sum(jnp.sum(t, axis=-1, keepdims=True) for t in gv) / width
    xc = [t - mu for t in gv]
    inv = lax.rsqrt(sum(jnp.sum(t * t, axis=-1, keepdims=True) for t in xc) / width + EPS)
    qi = lax.broadcasted_iota(jnp.int32, (CHUNK, CHUNK), 0)
    ki = lax.broadcasted_iota(jnp.int32, (CHUNK, CHUNK), 1)
    outs = []
    for g in range(G):
        vn = (xc[g] * inv * lnw_ref[g]).astype(BF16)
        wm = jnp.where(ki <= qi, ws_ref[g], 0.0).astype(BF16)
        sp = [jnp.dot(wm, vn[c * CHUNK:(c + 1) * CHUNK, :], preferred_element_type=F32) + bs_ref[g]
              for c in range(tm // CHUNK)]
        outs.append(jax.nn.gelu(u_refs[g][...].astype(F32)) * jnp.concatenate(sp, axis=0))
    inv_o = lax.rsqrt(sum(jnp.sum(t * t, axis=-1, keepdims=True) for t in outs) / width + EPS)
    for g in range(G):
        o_ref[g] = (outs[g] * inv_o * ow_ref[g]).astype(o_ref.dtype)


def _gmlp(proj, ln_w, ws, bs, out_w, tm=1024):
    S = proj.shape[1]
    G = N_GMLP
    slab = lambda s: pl.BlockSpec((None, tm, LANES), lambda i: (s, i, 0))
    small = lambda shape: pl.BlockSpec(shape, lambda i: (0,) * len(shape))
    return pl.pallas_call(
        _gmlp_kernel,
        grid=(S // tm,),
        in_specs=[slab(GU + g) for g in range(G)] + [slab(GV + g) for g in range(G)]
        + [small((G, 1, LANES)), small((G, CHUNK, CHUNK)), small((G, CHUNK, 1)), small((G, 1, LANES))],
        out_specs=pl.BlockSpec((G, tm, LANES), lambda i: (0, i, 0)),
        out_shape=jax.ShapeDtypeStruct((G, S, LANES), BF16),
        compiler_params=_params("parallel"),
        name="gmlp",
    )(*([proj] * (2 * G)), ln_w.reshape(G, 1, LANES), ws, bs.reshape(G, CHUNK, 1), out_w.reshape(G, 1, LANES))


def _outproj_kernel(ret_ref, att_ref, gm_ref, x_ref, attw_ref, wb, n2w_ref, x1_ref, h2_ref, a_scr, *, nc):
    col = lambda j: slice(j * LANES, (j + 1) * LANES)
    for j in range(N_RET):
        a_scr[:, col(j)] = ret_ref[j].astype(BF16)
    att = [att_ref[j] for j in range(N_ATT)]
    inv = lax.rsqrt(sum(jnp.sum(t * t, axis=-1, keepdims=True) for t in att) / (N_ATT * LANES) + EPS)
    for j in range(N_ATT):
        a_scr[:, col(N_RET + j)] = (att[j] * inv * attw_ref[j]).astype(BF16)
    for j in range(N_GMLP):
        a_scr[:, col(N_RET + N_ATT + j)] = gm_ref[j].astype(BF16)
    D = x_ref.shape[1]
    ssq = 0.0
    for c in range(D // nc):
        cs = slice(c * nc, (c + 1) * nc)
        x1 = x_ref[:, cs] + jnp.dot(a_scr[...], wb[:, cs], preferred_element_type=F32)
        x1_ref[:, cs] = x1
        ssq = ssq + jnp.sum(x1 * x1, axis=-1, keepdims=True)
    h2_ref[...] = (x1_ref[...] * lax.rsqrt(ssq / D + EPS) * n2w_ref[...]).astype(BF16)


def _outproj(ret, att, gm, x, att_w, w, n2w, tm=512, nc=512):
    S, D = x.shape
    K = w.shape[0]
    rows = lambda n: pl.BlockSpec((n, tm, LANES), lambda i: (0, i, 0))
    full = lambda: pl.BlockSpec((tm, D), lambda i: (i, 0))
    const = lambda shape: pl.BlockSpec(shape, lambda i: (0,) * len(shape))
    return pl.pallas_call(
        functools.partial(_outproj_kernel, nc=nc),
        grid=(S // tm,),
        in_specs=[rows(N_RET), rows(N_ATT), rows(N_GMLP), full(), const((N_ATT, 1, LANES)),
                  pl.BlockSpec((K, D), lambda i: (0, 0), pipeline_mode=pl.Buffered(1)), const((1, D))],
        out_specs=[full(), full()],
        out_shape=[jax.ShapeDtypeStruct((S, D), F32), jax.ShapeDtypeStruct((S, D), BF16)],
        scratch_shapes=[pltpu.VMEM((tm, K), BF16)],
        compiler_params=_params("parallel"),
        name="outproj",
    )(ret, att, gm, x, att_w.reshape(N_ATT, 1, LANES), w, n2w.reshape(1, D))


def _upproj_kernel(a_ref, wg_ref, wv_ref, cwg_ref, cwv_ref, cbg_ref, cbv_ref, wc_ref, o_ref, wcb_ref, wgb, wvb, hg, hv):
    wcb_ref[...] = wc_ref[...].astype(BF16)
    tm = a_ref.shape[0]
    nslab = hg.shape[0]
    head = SUBLANES

    @pl.when(pl.program_id(1) == 0)
    def _():
        wgb[...] = wg_ref[...].astype(BF16)
        wvb[...] = wv_ref[...].astype(BF16)
        hg[:, 0:head, :] = jnp.zeros((nslab, head, LANES), F32)
        hv[:, 0:head, :] = jnp.zeros((nslab, head, LANES), F32)

    a = a_ref[...]

    def conv(wb, cw_ref, cb_ref, hbuf):
        h = jnp.dot(a, wb[...], preferred_element_type=F32)
        outs = []
        for s in range(nslab):
            cs = slice(s * LANES, (s + 1) * LANES)
            hs = h[:, cs]
            hbuf[s, head:head + tm, :] = hs
            outs.append(cb_ref[:, cs] + cw_ref[0:1, cs] * hbuf[s, head - 2:head - 2 + tm, :]
                        + cw_ref[1:2, cs] * hbuf[s, head - 1:head - 1 + tm, :] + cw_ref[2:3, cs] * hs)
            hbuf[s, 0:head, :] = hbuf[s, tm:tm + head, :]
        return outs

    gate = conv(wgb, cwg_ref, cbg_ref, hg)
    val = conv(wvb, cwv_ref, cbv_ref, hv)
    for s in range(nslab):
        o_ref[:, s * LANES:(s + 1) * LANES] = (_silu(gate[s]) * val[s]).astype(o_ref.dtype)


def _upproj(h2, w_up, conv_w, conv_b, wc, l, tm=1024, tn=512):
    S, D = h2.shape
    F = w_up.shape[2] // 2
    nj, nm = F // tn, S // tm
    taps = conv_w.shape[1]
    hbuf = pltpu.VMEM((tn // LANES, tm + SUBLANES, LANES), F32)
    wc_in, wc_out, wc_shape = _sidecast(wc, l, nj * nm, lambda j, i: j * nm + i)
    return pl.pallas_call(
        _upproj_kernel,
        grid=(nj, nm),
        in_specs=[pl.BlockSpec((tm, D), lambda j, i: (i, 0)),
                  pl.BlockSpec((None, D, tn), lambda j, i: (l, 0, j)),
                  pl.BlockSpec((None, D, tn), lambda j, i: (l, 0, j + nj)),
                  pl.BlockSpec((None, taps, tn), lambda j, i: (l, 0, j)),
                  pl.BlockSpec((None, taps, tn), lambda j, i: (l, 0, j + nj)),
                  pl.BlockSpec((None, 1, tn), lambda j, i: (l, 0, j)),
                  pl.BlockSpec((None, 1, tn), lambda j, i: (l, 0, j + nj)),
                  wc_in],
        out_specs=[pl.BlockSpec((tm, tn), lambda j, i: (i, j)), wc_out],
        out_shape=[jax.ShapeDtypeStruct((S, F), BF16), wc_shape],
        scratch_shapes=[pltpu.VMEM((D, tn), BF16), pltpu.VMEM((D, tn), BF16), hbuf, hbuf],
        compiler_params=_params("parallel", "arbitrary"),
        name="upproj",
    )(h2, w_up, w_up, conv_w, conv_w, conv_b[:, None, :], conv_b[:, None, :], wc)


def _downproj_kernel(a_ref, w_ref, x1_ref, nw_ref, wc_ref, o_ref, xw_ref, ssq_ref, wcb_ref):
    x2 = x1_ref[...] + jnp.dot(a_ref[...], w_ref[...], preferred_element_type=F32)
    o_ref[...] = x2
    xw_ref[...] = (x2 * nw_ref[...]).astype(BF16)
    ssq_ref[...] = jnp.sum(x2 * x2, axis=-1, keepdims=True)
    wcb_ref[...] = wc_ref[...].astype(BF16)


def _downproj(g, w, x1, next_w, wc, lc, tm=512, tn=1024):
    S, F = g.shape
    D = w.shape[1]
    nm = S // tm
    tile = lambda: pl.BlockSpec((tm, tn), lambda j, i: (i, j))
    wc_in, wc_out, wc_shape = _sidecast(wc, lc, (D // tn) * nm, lambda j, i: j * nm + i)
    return pl.pallas_call(
        _downproj_kernel,
        grid=(D // tn, nm),
        in_specs=[pl.BlockSpec((tm, F), lambda j, i: (i, 0)), pl.BlockSpec((F, tn), lambda j, i: (0, j)), tile(),
                  pl.BlockSpec((1, tn), lambda j, i: (0, j)), wc_in],
        out_specs=[tile(), tile(), pl.BlockSpec((None, tm, 1), lambda j, i: (j, i, 0)), wc_out],
        out_shape=[jax.ShapeDtypeStruct((S, D), F32), jax.ShapeDtypeStruct((S, D), BF16),
                   jax.ShapeDtypeStruct((D // tn, S, 1), F32), wc_shape],
        compiler_params=_params("parallel", "parallel"),
        name="downproj",
    )(g, w, x1, next_w.reshape(1, D), wc)


def _downproj_final_kernel(a_ref, w_ref, x1_ref, nw_ref, o_ref, *, nc):
    D = o_ref.shape[1]
    a = a_ref[...]
    ssq = 0.0
    for c in range(D // nc):
        cs = slice(c * nc, (c + 1) * nc)
        x2 = x1_ref[:, cs] + jnp.dot(a, w_ref[:, cs], preferred_element_type=F32)
        o_ref[:, cs] = x2
        ssq = ssq + jnp.sum(x2 * x2, axis=-1, keepdims=True)
    o_ref[...] = o_ref[...] * lax.rsqrt(ssq / D + EPS) * nw_ref[...]


def _downproj_final(g, w, x1, final_w, tm=512, nc=512):
    S, F = g.shape
    D = w.shape[1]
    rows = lambda: pl.BlockSpec((tm, D), lambda i: (i, 0))
    return pl.pallas_call(
        functools.partial(_downproj_final_kernel, nc=nc),
        grid=(S // tm,),
        in_specs=[pl.BlockSpec((tm, F), lambda i: (i, 0)),
                  pl.BlockSpec((F, D), lambda i: (0, 0), pipeline_mode=pl.Buffered(1)),
                  rows(), pl.BlockSpec((1, D), lambda i: (0, 0))],
        out_specs=rows(),
        out_shape=jax.ShapeDtypeStruct((S, D), F32),
        compiler_params=_params("parallel"),
        name="downproj_final",
    )(g, w, x1, final_w.reshape(1, D))


def kernel(x, norm1_w, w_in, ret_norm_w, att_norm_w, gmlp_ln_w, gmlp_ws, gmlp_bs, gmlp_out_w, w_out, norm2_w, w_up, conv_w, conv_b, w_down, final_norm_w):
    B, S, D = x.shape
    depth = w_in.shape[0]
    assert B == 1 and S % SPAN == 0 and w_in.shape[2] == N_SLABS * LANES
    tables = _retention_tables(S)
    xs = x.reshape(S, D)
    for l in range(depth):
        if l == 0:
            proj, w_out_b = _inproj_first(xs, norm1_w[0], w_in, 0, w_out, 0)
        else:
            proj = _inproj(xw, ssq, w_in, l)
        ret = _retention(proj, tables, ret_norm_w[l])
        att = _dilated_attention(proj)
        gm = _gmlp(proj, gmlp_ln_w[l], gmlp_ws[l], gmlp_bs[l], gmlp_out_w[l])
        x1, h2 = _outproj(ret, att, gm, xs, att_norm_w[l], w_out_b, norm2_w[l])
        g, w_down_b = _upproj(h2, w_up, conv_w, conv_b, w_down, l)
        if l + 1 < depth:
            xs, xw, ssq, w_out_b = _downproj(g, w_down_b, x1, norm1_w[l + 1], w_out, l + 1)
        else:
            out = _downproj_final(g, w_down_b, x1, final_norm_w)
    return out.reshape(B, S, D)
```

```python
import functools
import math

import jax
import jax.numpy as jnp
from jax import lax
from jax.experimental import pallas as pl
from jax.experimental.pallas import tpu as pltpu

F32 = jnp.float32
BF16 = jnp.bfloat16

LANES = 128
SUBLANES = 8
BF16_SUBLANES = 16
CHUNK = 128
N_RET = 6
N_ATT = 6
N_GMLP = 4
DILATIONS = (1, 4, 16)
SPAN = CHUNK * DILATIONS[-1]
ROPE_BASE = 10000.0
EPS = 1e-6
NEG_INF = -1e30
VMEM_LIMIT = 56 * 1024 * 1024

RQ, RK, RV, RG = 0, N_RET, 2 * N_RET, 3 * N_RET
AQ = 4 * N_RET
AK = AQ + N_ATT
AV = AK + N_ATT
GU = AV + N_ATT
GV = GU + N_GMLP
N_SLABS = GV + N_GMLP


def _params(*sem):
    return pltpu.CompilerParams(dimension_semantics=sem, vmem_limit_bytes=VMEM_LIMIT)


def _sidecast(w, l, steps, step_of):
    _, rows, cols = w.shape
    band = rows // steps
    assert band * steps == rows and band % BF16_SUBLANES == 0
    return (pl.BlockSpec((None, band, cols), lambda *g: (l, step_of(*g), 0)),
            pl.BlockSpec((band, cols), lambda *g: (step_of(*g), 0)),
            jax.ShapeDtypeStruct((rows, cols), BF16))


def _rms(x, w):
    return x * lax.rsqrt(jnp.mean(x * x, axis=-1, keepdims=True) + EPS) * w


def _silu(x):
    hx = 0.5 * x
    return hx + hx * jnp.tanh(hx)


def _inproj_kernel(a_ref, ssq_ref, w_ref, o_ref, wb):
    @pl.when(pl.program_id(1) == 0)
    def _():
        wb[...] = w_ref[...].astype(BF16)

    D = a_ref.shape[1]
    r = lax.rsqrt(sum(ssq_ref[p] for p in range(ssq_ref.shape[0])) / D + EPS)
    acc = jnp.dot(a_ref[...], wb[...], preferred_element_type=F32)
    for j in range(o_ref.shape[0]):
        o_ref[j] = (acc[:, j * LANES:(j + 1) * LANES] * r).astype(o_ref.dtype)


def _inproj(xw, ssq, w, l, tm=1024, tn=1280):
    S, D = xw.shape
    N = w.shape[2]
    parts = ssq.shape[0]
    return pl.pallas_call(
        _inproj_kernel,
        grid=(N // tn, S // tm),
        in_specs=[pl.BlockSpec((tm, D), lambda j, i: (i, 0)), pl.BlockSpec((parts, tm, 1), lambda j, i: (0, i, 0)),
                  pl.BlockSpec((None, D, tn), lambda j, i: (l, 0, j))],
        out_specs=pl.BlockSpec((tn // LANES, tm, LANES), lambda j, i: (j, i, 0)),
        out_shape=jax.ShapeDtypeStruct((N // LANES, S, LANES), BF16),
        scratch_shapes=[pltpu.VMEM((D, tn), BF16)],
        compiler_params=_params("parallel", "arbitrary"),
        name="inproj",
    )(xw, ssq, w)


def _inproj_first_kernel(x_ref, nw_ref, w_ref, wc_ref, o_ref, wcb_ref, wb):
    @pl.when(pl.program_id(1) == 0)
    def _():
        wb[...] = w_ref[...].astype(BF16)

    wcb_ref[...] = wc_ref[...].astype(BF16)
    x = x_ref[...]
    r = lax.rsqrt(jnp.mean(x * x, axis=-1, keepdims=True) + EPS)
    acc = jnp.dot((x * nw_ref[...]).astype(BF16), wb[...], preferred_element_type=F32)
    for j in range(o_ref.shape[0]):
        o_ref[j] = (acc[:, j * LANES:(j + 1) * LANES] * r).astype(o_ref.dtype)


def _inproj_first(x, nw, w, l, wc, lc, tm=512, tn=1280):
    S, D = x.shape
    N = w.shape[2]
    nm = S // tm
    wc_in, wc_out, wc_shape = _sidecast(wc, lc, nm, lambda j, i: jnp.minimum(j * nm + i, nm - 1))
    return pl.pallas_call(
        _inproj_first_kernel,
        grid=(N // tn, nm),
        in_specs=[pl.BlockSpec((tm, D), lambda j, i: (i, 0)), pl.BlockSpec((1, D), lambda j, i: (0, 0)),
                  pl.BlockSpec((None, D, tn), lambda j, i: (l, 0, j)), wc_in],
        out_specs=[pl.BlockSpec((tn // LANES, tm, LANES), lambda j, i: (j, i, 0)), wc_out],
        out_shape=[jax.ShapeDtypeStruct((N // LANES, S, LANES), BF16), wc_shape],
        scratch_shapes=[pltpu.VMEM((D, tn), BF16)],
        compiler_params=_params("arbitrary", "arbitrary"),
        name="inproj_first",
    )(x, nw.reshape(1, D), w, wc)


def _retention_kernel(q_ref, k_ref, v_ref, g_ref, ra_ref, rb_ref, dec_ref, xi_ref, zeta_ref, cd_ref,
                      nw_ref, o_ref, state_ref, lbuf, ubuf, sbuf):
    @pl.when(pl.program_id(1) == 0)
    def _():
        state_ref[...] = jnp.zeros_like(state_ref)

    scale = LANES ** -0.5
    nh, tr = q_ref.shape[0], q_ref.shape[1]
    nc = tr // CHUNK
    cb, sb = rb_ref[0], rb_ref[1]
    for c in range(nc):
        rows = pl.ds(c * CHUNK, CHUNK)
        ca, sa, cas, sas = (ra_ref[t, pl.ds(c, 1), :] for t in range(4))
        cos = ca * cb - sa * sb
        sin = sas * cb + cas * sb
        for h in range(nh):
            u = c * nh + h
            q, k = q_ref[h, rows, :].astype(F32), k_ref[h, rows, :].astype(F32)
            qr = q * cos + pltpu.roll(q, LANES // 2, 1) * sin
            kr = (k * cos + pltpu.roll(k, LANES // 2, 1) * sin) * scale
            vb = v_ref[h, rows, :].astype(BF16)
            s = lax.dot_general(qr.astype(BF16), kr.astype(BF16), (((1,), (1,)), ((), ())),
                                preferred_element_type=F32) * dec_ref[h]
            lbuf[u, :, 0:CHUNK] = s.astype(BF16)
            lbuf[u, :, CHUNK:CHUNK + LANES] = (qr * xi_ref[h]).astype(BF16)
            ubuf[u] = lax.dot_general((kr * zeta_ref[h]).astype(BF16), vb, (((0,), (0,)), ((), ())),
                                      preferred_element_type=F32)
    for h in range(nh):
        state = state_ref[h]
        for c in range(nc):
            sbuf[c * nh + h] = state.astype(BF16)
            state = cd_ref[h] * state + ubuf[c * nh + h]
        state_ref[h] = state
    for c in range(nc):
        rows = pl.ds(c * CHUNK, CHUNK)
        for h in range(nh):
            u = c * nh + h
            rhs = jnp.concatenate([v_ref[h, rows, :].astype(BF16), sbuf[u]], axis=0)
            o = jnp.dot(lbuf[u], rhs, preferred_element_type=F32)
            o_ref[h, rows, :] = (_silu(g_ref[h, rows, :].astype(F32)) * _rms(o, nw_ref[h])).astype(o_ref.dtype)


def _retention_tables(S):
    half = LANES // 2
    inv_freq = ROPE_BASE ** (-jnp.arange(half, dtype=F32) / half)
    inv_freq = jnp.concatenate([inv_freq, inv_freq])
    sign = jnp.concatenate([-jnp.ones(half, F32), jnp.ones(half, F32)])
    base = jnp.arange(0, S, CHUNK, dtype=jnp.int32).astype(F32)[:, None] * inv_freq[None, :]
    offs = jnp.arange(CHUNK, dtype=jnp.int32).astype(F32)[:, None] * inv_freq[None, :]
    rot_a = jnp.stack([jnp.cos(base), jnp.sin(base), sign * jnp.cos(base), sign * jnp.sin(base)])
    rot_b = jnp.stack([jnp.cos(offs), jnp.sin(offs)])
    log_g = jnp.log1p(-jnp.exp2(-5.0 - jnp.arange(N_RET, dtype=F32)))
    idx = jnp.arange(CHUNK, dtype=F32)
    diff = idx[:, None] - idx[None, :]
    dec = jnp.where(diff[None] >= 0, jnp.exp(jnp.maximum(diff, 0.0)[None] * log_g[:, None, None]), 0.0)
    xi = jnp.exp((idx[None, :] + 1.0) * log_g[:, None])
    zeta = jnp.exp((CHUNK - 1.0 - idx)[None, :] * log_g[:, None])
    cd = jnp.exp(CHUNK * log_g)
    bcast = lambda t: jnp.broadcast_to(t[:, :, None], (N_RET, CHUNK, LANES))
    return rot_a, rot_b, dec, bcast(xi), bcast(zeta), jnp.broadcast_to(cd[:, None, None], (N_RET, 1, LANES))


def _retention(proj, tables, norm_w, tr=1024, nh=N_RET):
    S = proj.shape[1]
    rot_a, rot_b, dec, xi, zeta, cd = tables
    assert all(off % nh == 0 for off in (RQ, RK, RV, RG)) and N_RET % nh == 0
    slab = lambda off: pl.BlockSpec((nh, tr, LANES), lambda h, i: (off // nh + h, i, 0))
    per_head = lambda r: pl.BlockSpec((nh, r, LANES), lambda h, i: (h, 0, 0))
    return pl.pallas_call(
        _retention_kernel,
        grid=(N_RET // nh, S // tr),
        in_specs=[slab(RQ), slab(RK), slab(RV), slab(RG),
                  pl.BlockSpec((4, tr // CHUNK, LANES), lambda h, i: (0, i, 0)),
                  pl.BlockSpec((2, CHUNK, LANES), lambda h, i: (0, 0, 0)),
                  per_head(CHUNK), per_head(CHUNK), per_head(CHUNK), per_head(1), per_head(1)],
        out_specs=pl.BlockSpec((nh, tr, LANES), lambda h, i: (h, i, 0)),
        out_shape=jax.ShapeDtypeStruct((N_RET, S, LANES), BF16),
        scratch_shapes=[pltpu.VMEM((nh, LANES, LANES), F32),
                        pltpu.VMEM((nh * tr // CHUNK, CHUNK, CHUNK + LANES), BF16),
                        pltpu.VMEM((nh * tr // CHUNK, LANES, LANES), F32),
                        pltpu.VMEM((nh * tr // CHUNK, LANES, LANES), BF16)],
        compiler_params=_params("parallel", "arbitrary"),
        name="retention",
    )(proj, proj, proj, proj, rot_a, rot_b, dec, xi, zeta, cd, norm_w.reshape(N_RET, 1, LANES))


D2 = DILATIONS[1]
TROWS = SPAN // D2
UNITS = 3 * (SPAN // CHUNK)
assert DILATIONS == (1, D2, D2 * D2) and SPAN == CHUNK * D2 * D2


def _dilated_kernel(q_in, k_in, v_in, o_ref, q_ref, k_ref, v_ref, tq, tk, tv, kt, vt, sbuf,
                    o1, d1, m1, o2, d2, m2, o3, d3, m3):
    n = pl.program_id(1)
    slot = n % 2
    pslot = 1 - slot
    nblk = SPAN // CHUNK
    scale2 = LANES ** -0.5 * math.log2(math.e)
    blk = lambda i: pl.ds(i * CHUNK, CHUNK)
    strided = lambda start: pl.ds(start, CHUNK, stride=D2)

    @pl.when(n == 0)
    def _():
        tk[1] = jnp.zeros(tk.shape[1:], F32)
        tv[1] = jnp.zeros(tv.shape[1:], F32)
        kt[...] = jnp.zeros(kt.shape, BF16)
        vt[...] = jnp.zeros(vt.shape, BF16)

    q_ref[...] = q_in[...].astype(F32)
    k_ref[...] = k_in[...].astype(F32)
    v_ref[...] = v_in[...].astype(F32)
    for r in range(D2):
        tq[r] = q_ref[pl.ds(r, TROWS, stride=D2), :]
        tk[slot, r] = k_ref[pl.ds(r, TROWS, stride=D2), :]
        tv[slot, r] = v_ref[pl.ds(r, TROWS, stride=D2), :]

    qi = lax.broadcasted_iota(jnp.int32, (CHUNK, CHUNK), 0)
    ci = lax.broadcasted_iota(jnp.int32, (CHUNK, CHUNK), 1)
    own_ok = ci <= qi
    prev_ok = ci >= qi
    prev_ok_span = ci >= qi + jnp.where(n > 0, 0, CHUNK)

    def units(it):
        ss, r = it // D2, it % D2
        narrow_prev = (lambda kv: kv[2][...]) if it == 0 else (lambda kv: kv[0][blk(it - 1), :])
        mid_prev = ((lambda kv: kv[1][slot, r, blk(ss - 1), :]) if ss > 0 else
                    (lambda kv: kv[1][pslot, r, blk(D2 - 1), :]))
        return (
            (lambda: q_in[blk(it), :], narrow_prev, lambda kv: kv[0][blk(it), :],
             prev_ok if it > 0 else prev_ok_span, (blk(it), slice(None))),
            (lambda: tq[r, blk(ss), :], mid_prev, lambda kv: kv[1][slot, r, blk(ss), :],
             prev_ok if ss > 0 else prev_ok_span, (r, blk(ss), slice(None))),
            (lambda: tq[r, strided(ss), :], lambda kv: kv[1][pslot, r, strided(ss), :],
             lambda kv: kv[1][slot, r, strided(ss), :], prev_ok_span, (r, strided(ss), slice(None))),
        )

    keys = (k_in, tk, kt)
    vals = (v_in, tv, vt)

    for it in range(nblk):
        for p, (load_q, load_prev, load_own, mask_prev, rows), m_s in zip(range(3), units(it), (m1, m2, m3)):
            k = jnp.concatenate([load_prev(keys), load_own(keys)], axis=0).astype(BF16)
            s = lax.dot_general(load_q().astype(BF16), k, (((1,), (1,)), ((), ())),
                                preferred_element_type=F32)
            sp = jnp.where(mask_prev, s[:, :CHUNK], NEG_INF)
            so = jnp.where(own_ok, s[:, CHUNK:], NEG_INF)
            sbuf[3 * it + p, :, 0:CHUNK] = sp
            sbuf[3 * it + p, :, CHUNK:2 * CHUNK] = so
            m_s[rows] = jnp.broadcast_to(jnp.max(jnp.maximum(sp, so), axis=-1, keepdims=True), (CHUNK, LANES))

    ones = jnp.ones((2 * CHUNK, LANES), BF16)
    for it in range(nblk):
        for p, (_, load_prev, load_own, _, rows), o_s, d_s, m_s in zip(
                range(3), units(it), (o1, o2, o3), (d1, d2, d3), (m1, m2, m3)):
            m = m_s[rows]
            e = jnp.concatenate([jnp.exp2((sbuf[3 * it + p, :, 0:CHUNK] - m) * scale2),
                                 jnp.exp2((sbuf[3 * it + p, :, CHUNK:2 * CHUNK] - m) * scale2)], axis=1)
            v = jnp.concatenate([load_prev(vals), load_own(vals)], axis=0).astype(BF16)
            oa = jnp.dot(e.astype(BF16), jnp.concatenate([v, ones], axis=1), preferred_element_type=F32)
            o_s[rows] = oa[:, :LANES]
            d_s[rows] = oa[:, LANES:]

    kt[...] = k_in[blk(nblk - 1), :]
    vt[...] = v_in[blk(nblk - 1), :]

    def phase_c(j, carry):
        for r in range(D2):
            nat = (pl.ds(j * CHUNK * D2 + r, CHUNK, stride=D2), slice(None))
            trw = (r, pl.ds(j * CHUNK, CHUNK), slice(None))
            ma, mb, mc = m1[nat], m2[trw], m3[trw]
            mx = jnp.maximum(jnp.maximum(ma, mb), mc)
            wa, wb, wc = (jnp.exp2((t - mx) * scale2) for t in (ma, mb, mc))
            num = wa * o1[nat] + wb * o2[trw] + wc * o3[trw]
            den = wa * d1[nat] + wb * d2[trw] + wc * d3[trw]
            o_ref[nat] = num / den
        return carry

    lax.fori_loop(0, TROWS // CHUNK, phase_c, 0)


def _dilated_attention(proj):
    S = proj.shape[1]
    slab = lambda off: pl.BlockSpec((None, SPAN, LANES), lambda h, n: (off + h, n, 0))
    nat = pltpu.VMEM((SPAN, LANES), F32)
    res = pltpu.VMEM((D2, TROWS, LANES), F32)
    ring = pltpu.VMEM((2, D2, TROWS, LANES), F32)
    tail = pltpu.VMEM((CHUNK, LANES), BF16)
    return pl.pallas_call(
        _dilated_kernel,
        grid=(N_ATT, S // SPAN),
        in_specs=[slab(AQ), slab(AK), slab(AV)],
        out_specs=pl.BlockSpec((None, SPAN, LANES), lambda h, n: (h, n, 0)),
        out_shape=jax.ShapeDtypeStruct((N_ATT, S, LANES), F32),
        scratch_shapes=[nat, nat, nat, res, ring, ring, tail, tail, pltpu.VMEM((UNITS, CHUNK, 2 * CHUNK), F32),
                        nat, nat, nat, res, res, res, res, res, res],
        compiler_params=_params("parallel", "arbitrary"),
        name="dilated_attention",
    )(proj, proj, proj)


def _gmlp_kernel(*refs):
    G = N_GMLP
    u_refs, v_refs = refs[:G], refs[G:2 * G]
    lnw_ref, ws_ref, bs_ref, ow_ref, o_ref = refs[2 * G:]
    tm = o_ref.shape[1]
    width = G * LANES
    gv = [jax.nn.gelu(r[...].astype(F32)) for r in v_refs]
    mu = sum(jnp.sum(t, axis=-1, keepdims=True) for t in gv) / width
    xc = [t - mu for t in gv]
    inv = lax.rsqrt(sum(jnp.sum(t * t, axis=-1, keepdims=True) for t in xc) / width + EPS)
    qi = lax.broadcasted_iota(jnp.int32, (CHUNK, CHUNK), 0)
    ki = lax.broadcasted_iota(jnp.int32, (CHUNK, CHUNK), 1)
    outs = []
    for g in range(G):
        vn = (xc[g] * inv * lnw_ref[g]).astype(BF16)
        wm = jnp.where(ki <= qi, ws_ref[g], 0.0).astype(BF16)
        sp = [jnp.dot(wm, vn[c * CHUNK:(c + 1) * CHUNK, :], preferred_element_type=F32) + bs_ref[g]
              for c in range(tm // CHUNK)]
        outs.append(jax.nn.gelu(u_refs[g][...].astype(F32)) * jnp.concatenate(sp, axis=0))
    inv_o = lax.rsqrt(sum(jnp.sum(t * t, axis=-1, keepdims=True) for t in outs) / width + EPS)
    for g in range(G):
        o_ref[g] = (outs[g] * inv_o * ow_ref[g]).astype(o_ref.dtype)


def _gmlp(proj, ln_w, ws, bs, out_w, tm=1024):
    S = proj.shape[1]
    G = N_GMLP
    slab = lambda s: pl.BlockSpec((None, tm, LANES), lambda i: (s, i, 0))
    small = lambda shape: pl.BlockSpec(shape, lambda i: (0,) * len(shape))
    return pl.pallas_call(
        _gmlp_kernel,
        grid=(S // tm,),
        in_specs=[slab(GU + g) for g in range(G)] + [slab(GV + g) for g in range(G)]
        + [small((G, 1, LANES)), small((G, CHUNK, CHUNK)), small((G, CHUNK, 1)), small((G, 1, LANES))],
        out_specs=pl.BlockSpec((G, tm, LANES), lambda i: (0, i, 0)),
        out_shape=jax.ShapeDtypeStruct((G, S, LANES), BF16),
        compiler_params=_params("parallel"),
        name="gmlp",
    )(*([proj] * (2 * G)), ln_w.reshape(G, 1, LANES), ws, bs.reshape(G, CHUNK, 1), out_w.reshape(G, 1, LANES))


def _outproj_kernel(ret_ref, att_ref, gm_ref, x_ref, attw_ref, wb, n2w_ref, x1_ref, h2_ref, a_scr, *, nc):
    col = lambda j: slice(j * LANES, (j + 1) * LANES)
    for j in range(N_RET):
        a_scr[:, col(j)] = ret_ref[j].astype(BF16)
    att = [att_ref[j] for j in range(N_ATT)]
    inv = lax.rsqrt(sum(jnp.sum(t * t, axis=-1, keepdims=True) for t in att) / (N_ATT * LANES) + EPS)
    for j in range(N_ATT):
        a_scr[:, col(N_RET + j)] = (att[j] * inv * attw_ref[j]).astype(BF16)
    for j in range(N_GMLP):
        a_scr[:, col(N_RET + N_ATT + j)] = gm_ref[j].astype(BF16)
    D = x_ref.shape[1]
    ssq = 0.0
    for c in range(D // nc):
        cs = slice(c * nc, (c + 1) * nc)
        x1 = x_ref[:, cs] + jnp.dot(a_scr[...], wb[:, cs], preferred_element_type=F32)
        x1_ref[:, cs] = x1
        ssq = ssq + jnp.sum(x1 * x1, axis=-1, keepdims=True)
    h2_ref[...] = (x1_ref[...] * lax.rsqrt(ssq / D + EPS) * n2w_ref[...]).astype(BF16)


def _outproj(ret, att, gm, x, att_w, w, n2w, tm=512, nc=512):
    S, D = x.shape
    K = w.shape[0]
    rows = lambda n: pl.BlockSpec((n, tm, LANES), lambda i: (0, i, 0))
    full = lambda: pl.BlockSpec((tm, D), lambda i: (i, 0))
    const = lambda shape: pl.BlockSpec(shape, lambda i: (0,) * len(shape))
    return pl.pallas_call(
        functools.partial(_outproj_kernel, nc=nc),
        grid=(S // tm,),
        in_specs=[rows(N_RET), rows(N_ATT), rows(N_GMLP), full(), const((N_ATT, 1, LANES)),
                  pl.BlockSpec((K, D), lambda i: (0, 0), pipeline_mode=pl.Buffered(1)), const((1, D))],
        out_specs=[full(), full()],
        out_shape=[jax.ShapeDtypeStruct((S, D), F32), jax.ShapeDtypeStruct((S, D), BF16)],
        scratch_shapes=[pltpu.VMEM((tm, K), BF16)],
        compiler_params=_params("parallel"),
        name="outproj",
    )(ret, att, gm, x, att_w.reshape(N_ATT, 1, LANES), w, n2w.reshape(1, D))


def _upproj_kernel(a_ref, wg_ref, wv_ref, cwg_ref, cwv_ref, cbg_ref, cbv_ref, wc_ref, o_ref, wcb_ref, wgb, wvb, hg, hv):
    wcb_ref[...] = wc_ref[...].astype(BF16)
    tm = a_ref.shape[0]
    nslab = hg.shape[0]
    head = SUBLANES

    @pl.when(pl.program_id(1) == 0)
    def _():
        wgb[...] = wg_ref[...].astype(BF16)
        wvb[...] = wv_ref[...].astype(BF16)
        hg[:, 0:head, :] = jnp.zeros((nslab, head, LANES), F32)
        hv[:, 0:head, :] = jnp.zeros((nslab, head, LANES), F32)

    a = a_ref[...]

    def conv(wb, cw_ref, cb_ref, hbuf):
        h = jnp.dot(a, wb[...], preferred_element_type=F32)
        outs = []
        for s in range(nslab):
            cs = slice(s * LANES, (s + 1) * LANES)
            hs = h[:, cs]
            hbuf[s, head:head + tm, :] = hs
            outs.append(cb_ref[:, cs] + cw_ref[0:1, cs] * hbuf[s, head - 2:head - 2 + tm, :]
                        + cw_ref[1:2, cs] * hbuf[s, head - 1:head - 1 + tm, :] + cw_ref[2:3, cs] * hs)
            hbuf[s, 0:head, :] = hbuf[s, tm:tm + head, :]
        return outs

    gate = conv(wgb, cwg_ref, cbg_ref, hg)
    val = conv(wvb, cwv_ref, cbv_ref, hv)
    for s in range(nslab):
        o_ref[:, s * LANES:(s + 1) * LANES] = (_silu(gate[s]) * val[s]).astype(o_ref.dtype)


def _upproj(h2, w_up, conv_w, conv_b, wc, l, tm=1024, tn=512):
    S, D = h2.shape
    F = w_up.shape[2] // 2
    nj, nm = F // tn, S // tm
    taps = conv_w.shape[1]
    hbuf = pltpu.VMEM((tn // LANES, tm + SUBLANES, LANES), F32)
    wc_in, wc_out, wc_shape = _sidecast(wc, l, nj * nm, lambda j, i: j * nm + i)
    return pl.pallas_call(
        _upproj_kernel,
        grid=(nj, nm),
        in_specs=[pl.BlockSpec((tm, D), lambda j, i: (i, 0)),
                  pl.BlockSpec((None, D, tn), lambda j, i: (l, 0, j)),
                  pl.BlockSpec((None, D, tn), lambda j, i: (l, 0, j + nj)),
                  pl.BlockSpec((None, taps, tn), lambda j, i: (l, 0, j)),
                  pl.BlockSpec((None, taps, tn), lambda j, i: (l, 0, j + nj)),
                  pl.BlockSpec((None, 1, tn), lambda j, i: (l, 0, j)),
                  pl.BlockSpec((None, 1, tn), lambda j, i: (l, 0, j + nj)),
                  wc_in],
        out_specs=[pl.BlockSpec((tm, tn), lambda j, i: (i, j)), wc_out],
        out_shape=[jax.ShapeDtypeStruct((S, F), BF16), wc_shape],
        scratch_shapes=[pltpu.VMEM((D, tn), BF16), pltpu.VMEM((D, tn), BF16), hbuf, hbuf],
        compiler_params=_params("parallel", "arbitrary"),
        name="upproj",
    )(h2, w_up, w_up, conv_w, conv_w, conv_b[:, None, :], conv_b[:, None, :], wc)


def _downproj_kernel(a_ref, w_ref, x1_ref, nw_ref, wc_ref, o_ref, xw_ref, ssq_ref, wcb_ref):
    x2 = x1_ref[...] + jnp.dot(a_ref[...], w_ref[...], preferred_element_type=F32)
    o_ref[...] = x2
    xw_ref[...] = (x2 * nw_ref[...]).astype(BF16)
    ssq_ref[...] = jnp.sum(x2 * x2, axis=-1, keepdims=True)
    wcb_ref[...] = wc_ref[...].astype(BF16)


def _downproj(g, w, x1, next_w, wc, lc, tm=512, tn=1024):
    S, F = g.shape
    D = w.shape[1]
    nm = S // tm
    tile = lambda: pl.BlockSpec((tm, tn), lambda j, i: (i, j))
    wc_in, wc_out, wc_shape = _sidecast(wc, lc, (D // tn) * nm, lambda j, i: j * nm + i)
    return pl.pallas_call(
        _downproj_kernel,
        grid=(D // tn, nm),
        in_specs=[pl.BlockSpec((tm, F), lambda j, i: (i, 0)), pl.BlockSpec((F, tn), lambda j, i: (0, j)), tile(),
                  pl.BlockSpec((1, tn), lambda j, i: (0, j)), wc_in],
        out_specs=[tile(), tile(), pl.BlockSpec((None, tm, 1), lambda j, i: (j, i, 0)), wc_out],
        out_shape=[jax.ShapeDtypeStruct((S, D), F32), jax.ShapeDtypeStruct((S, D), BF16),
                   jax.ShapeDtypeStruct((D // tn, S, 1), F32), wc_shape],
        compiler_params=_params("parallel", "parallel"),
        name="downproj",
    )(g, w, x1, next_w.reshape(1, D), wc)


def _downproj_final_kernel(a_ref, w_ref, x1_ref, nw_ref, o_ref, *, nc):
    D = o_ref.shape[1]
    a = a_ref[...]
    ssq = 0.0
    for c in range(D // nc):
        cs = slice(c * nc, (c + 1) * nc)
        x2 = x1_ref[:, cs] + jnp.dot(a, w_ref[:, cs], preferred_element_type=F32)
        o_ref[:, cs] = x2
        ssq = ssq + jnp.sum(x2 * x2, axis=-1, keepdims=True)
    o_ref[...] = o_ref[...] * lax.rsqrt(ssq / D + EPS) * nw_ref[...]


def _downproj_final(g, w, x1, final_w, tm=512, nc=512):
    S, F = g.shape
    D = w.shape[1]
    rows = lambda: pl.BlockSpec((tm, D), lambda i: (i, 0))
    return pl.pallas_call(
        functools.partial(_downproj_final_kernel, nc=nc),
        grid=(S // tm,),
        in_specs=[pl.BlockSpec((tm, F), lambda i: (i, 0)),
                  pl.BlockSpec((F, D), lambda i: (0, 0), pipeline_mode=pl.Buffered(1)),
                  rows(), pl.BlockSpec((1, D), lambda i: (0, 0))],
        out_specs=rows(),
        out_shape=jax.ShapeDtypeStruct((S, D), F32),
        compiler_params=_params("parallel"),
        name="downproj_final",
    )(g, w, x1, final_w.reshape(1, D))


def kernel(x, norm1_w, w_in, ret_norm_w, att_norm_w, gmlp_ln_w, gmlp_ws, gmlp_bs, gmlp_out_w, w_out, norm2_w, w_up, conv_w, conv_b, w_down, final_norm_w):
    B, S, D = x.shape
    depth = w_in.shape[0]
    assert B == 1 and S % SPAN == 0 and w_in.shape[2] == N_SLABS * LANES
    tables = _retention_tables(S)
    xs = x.reshape(S, D)
    for l in range(depth):
        if l == 0:
            proj, w_out_b = _inproj_first(xs, norm1_w[0], w_in, 0, w_out, 0)
        else:
            proj = _inproj(xw, ssq, w_in, l)
        ret = _retention(proj, tables, ret_norm_w[l])
        att = _dilated_attention(proj)
        gm = _gmlp(proj, gmlp_ln_w[l], gmlp_ws[l], gmlp_bs[l], gmlp_out_w[l])
        x1, h2 = _outproj(ret, att, gm, xs, att_norm_w[l], w_out_b, norm2_w[l])
        g, w_down_b = _upproj(h2, w_up, conv_w, conv_b, w_down, l)
        if l + 1 < depth:
            xs, xw, ssq, w_out_b = _downproj(g, w_down_b, x1, norm1_w[l + 1], w_out, l + 1)
        else:
            out = _downproj_final(g, w_down_b, x1, final_norm_w)
    return out.reshape(B, S, D)
```
